```python
import math, functools
import jax, jax.numpy as jnp
from jax import lax
import numpy as np

D_MODEL = 4096
BATCH = 1
SEQ = 8192
DEPTH = 2

GRID_W = 64
CTX_LEN = 256
EPS = 1e-6
NEG_INF = -1e30
F32 = jnp.float32

HEAD_DIM = 128
A_HEADS = D_MODEL // 256
A_KV_HEADS = A_HEADS // 4
A_WINDOW = 128
A_BLOCK = 128
ROPE_BASE = 10000.0
ROPE_PAIRS = HEAD_DIM // 4
A_Q_WIDTH = A_HEADS * HEAD_DIM
A_KV_WIDTH = A_KV_HEADS * HEAD_DIM
B_WIDTH = D_MODEL // 2
POOL_WINDOWS = (2, 4, 8, 16)
POOL_GROUP = B_WIDTH // len(POOL_WINDOWS)
AB_IN = A_Q_WIDTH + 2 * A_KV_WIDTH + B_WIDTH
AB_MIX = A_Q_WIDTH + B_WIDTH
C_WIDTH = D_MODEL // 4
S5_H = 16
S5_GROUPS = C_WIDTH // S5_H
S5_P = 64
DT_MIN = 0.001
DT_MAX = 0.1
D_WIDTH = D_MODEL - C_WIDTH
SG_HEADS = 8
SG_GROUP = D_WIDTH // SG_HEADS
SG_CHUNK = 128
CD_IN = C_WIDTH + 2 * D_WIDTH
CD_MIX = C_WIDTH + D_WIDTH
MOE_GROUPS = 4
MOE_PER_GROUP = 8
MOE_EXPERTS = MOE_GROUPS * MOE_PER_GROUP
MOE_TOPK = 2
MOE_FF = D_MODEL // 8
N_EVEN = (DEPTH + 1) // 2
N_ODD = DEPTH // 2

kernel_name = 'hybrid_diffusion_trunk'


def rms_norm(x, g):
    xf = x.astype(F32)
    y = xf * lax.rsqrt(jnp.mean(xf * xf, axis=-1, keepdims=True) + EPS)
    return (y * g.astype(F32)).astype(x.dtype)


def layer_norm(x, g, b):
    xf = x.astype(F32)
    xc = xf - jnp.mean(xf, axis=-1, keepdims=True)
    y = xc * lax.rsqrt(jnp.mean(xc * xc, axis=-1, keepdims=True) + EPS)
    return (y * g.astype(F32) + b.astype(F32)).astype(x.dtype)


def modulation(cvec, mod_w, mod_b):
    return jnp.split(jax.nn.silu(cvec) @ mod_w + mod_b, 6, axis=-1)


def modulate(x, g, shift, scale):
    return rms_norm(x, g) * (1 + scale) + shift


def axial_rope_tables(n_tokens, dtype):
    rows = n_tokens // GRID_W
    t = jnp.arange(rows * GRID_W)
    row = (t // GRID_W).astype(F32)
    col = (t % GRID_W).astype(F32)
    inv_freq = ROPE_BASE ** (-jnp.arange(ROPE_PAIRS, dtype=F32) / ROPE_PAIRS)
    ang_r = row[:, None] * inv_freq
    ang_c = col[:, None] * inv_freq
    ang = jnp.concatenate([ang_r, ang_r, ang_c, ang_c], axis=-1)
    return jnp.cos(ang).astype(dtype), jnp.sin(ang).astype(dtype)


def apply_rope(x, cos, sin):
    xs = x.reshape(x.shape[:-1] + (2, 2, ROPE_PAIRS))
    rot = jnp.concatenate([-xs[..., 1:2, :], xs[..., 0:1, :]], axis=-2).reshape(x.shape)
    return x * cos[:, None, :] + rot * sin[:, None, :]


def attend_context_only(q, k, v, sink):
    B, Lc = q.shape[:2]
    G = A_HEADS // A_KV_HEADS
    qg = q.reshape(B, Lc, A_KV_HEADS, G, HEAD_DIM)
    s = jnp.einsum('bqhgd,bkhd->bhgqk', qg, k).astype(F32) * (HEAD_DIM ** -0.5)
    sk = jnp.broadcast_to(sink.astype(F32).reshape(1, A_KV_HEADS, G, 1, 1), s.shape[:-1] + (1,))
    p = jax.nn.softmax(jnp.concatenate([s, sk], axis=-1), axis=-1)[..., :-1].astype(v.dtype)
    o = jnp.einsum('bhgqk,bkhd->bqhgd', p, v)
    return o.reshape(B, Lc, A_Q_WIDTH)


def attend_latent_window(q, k, v, k_ctx, v_ctx, sink):
    B, L = q.shape[:2]
    G = A_HEADS // A_KV_HEADS
    nb = L // A_BLOCK
    nw = 3 * A_BLOCK
    qb = q.reshape(B, nb, A_BLOCK, A_KV_HEADS, G, HEAD_DIM)

    def band(t):
        tp = jnp.pad(t, ((0, 0), (A_BLOCK, A_BLOCK), (0, 0), (0, 0)))
        tp = tp.reshape(B, nb + 2, A_BLOCK, A_KV_HEADS, HEAD_DIM)
        return jnp.concatenate([tp[:, :-2], tp[:, 1:-1], tp[:, 2:]], axis=2)

    kw, vw = band(k), band(v)
    scale = HEAD_DIM ** -0.5
    s_win = jnp.einsum('bnqhgd,bnkhd->bnhgqk', qb, kw).astype(F32) * scale
    blk = jnp.arange(nb)[:, None]
    q_pos = blk * A_BLOCK + jnp.arange(A_BLOCK)[None, :]
    k_pos = (blk - 1) * A_BLOCK + jnp.arange(nw)[None, :]
    valid = ((jnp.abs(q_pos[:, :, None] - k_pos[:, None, :]) <= A_WINDOW)
             & (k_pos[:, None, :] >= 0) & (k_pos[:, None, :] < L))
    s_win = jnp.where(valid[None, :, None, None], s_win, NEG_INF)
    s_ctx = jnp.einsum('bnqhgd,bchd->bnhgqc', qb, k_ctx).astype(F32) * scale
    sk = jnp.broadcast_to(sink.astype(F32).reshape(1, 1, A_KV_HEADS, G, 1, 1), s_win.shape[:-1] + (1,))
    p = jax.nn.softmax(jnp.concatenate([s_win, s_ctx, sk], axis=-1), axis=-1)
    Lc = k_ctx.shape[1]
    p_win = p[..., :nw].astype(v.dtype)
    p_ctx = p[..., nw:nw + Lc].astype(v.dtype)
    o = (jnp.einsum('bnhgqk,bnkhd->bnqhgd', p_win, vw)
         + jnp.einsum('bnhgqc,bchd->bnqhgd', p_ctx, v_ctx))
    return o.reshape(B, L, A_Q_WIDTH)


def multiscale_pool(x):
    L = x.shape[1]
    xf = x.astype(F32)
    cs = jnp.pad(jnp.cumsum(xf, axis=1), ((0, 0), (1, 0), (0, 0)))
    t = jnp.arange(L)
    outs = []
    for gi, w in enumerate(POOL_WINDOWS):
        lo = jnp.clip(t - w // 2, 0, L)
        hi = jnp.clip(t - w // 2 + w, 0, L)
        sl = slice(gi * POOL_GROUP, (gi + 1) * POOL_GROUP)
        cg = cs[:, :, sl]
        mean = (cg[:, hi] - cg[:, lo]) / (hi - lo).astype(F32)[None, :, None]
        outs.append(mean - xf[:, :, sl])
    return jnp.concatenate(outs, axis=-1)


def pool_mixer(x, w_pool, pool_scale):
    d = multiscale_pool(x)
    B, L, _ = d.shape
    y = jnp.einsum('blgc,gce->blge', d.reshape(B, L, len(POOL_WINDOWS), POOL_GROUP), w_pool.astype(F32))
    return (y.reshape(B, L, B_WIDTH) * pool_scale.astype(F32)).astype(x.dtype)


def mixer_ab(proj, n_ctx, with_ctx_out, q_g, k_g, sink, w_pool, pool_scale):
    B, T, _ = proj.shape
    q, k, v, pl = jnp.split(proj, [A_Q_WIDTH, A_Q_WIDTH + A_KV_WIDTH, A_Q_WIDTH + 2 * A_KV_WIDTH], axis=-1)
    q = rms_norm(q.reshape(B, T, A_HEADS, HEAD_DIM), q_g)
    k = rms_norm(k.reshape(B, T, A_KV_HEADS, HEAD_DIM), k_g)
    v = v.reshape(B, T, A_KV_HEADS, HEAD_DIM)
    k_c, v_c = k[:, :n_ctx], v[:, :n_ctx]
    cos, sin = axial_rope_tables(T - n_ctx, q.dtype)
    q_l = apply_rope(q[:, n_ctx:], cos, sin)
    k_l = apply_rope(k[:, n_ctx:], cos, sin)
    a_lat = attend_latent_window(q_l, k_l, v[:, n_ctx:], k_c, v_c, sink)
    b_lat = pool_mixer(pl[:, n_ctx:], w_pool, pool_scale)
    out_l = jnp.concatenate([a_lat, b_lat], axis=-1)
    if not with_ctx_out:
        return None, out_l
    a_ctx = attend_context_only(q[:, :n_ctx], k_c, v_c, sink)
    b_ctx = pool_mixer(pl[:, :n_ctx], w_pool, pool_scale)
    return jnp.concatenate([a_ctx, b_ctx], axis=-1), out_l


def s5_discretize(lam_re, lam_im, log_dt, b_re, b_im):
    dt = jnp.exp(log_dt.astype(F32))[:, None]
    lr, li = lam_re.astype(F32), lam_im.astype(F32)
    mag = jnp.exp(lr * dt)
    a_re, a_im = mag * jnp.cos(li * dt), mag * jnp.sin(li * dt)
    den = lr * lr + li * li
    n_re = a_re - 1.0
    k_re = (n_re * lr + a_im * li) / den
    k_im = (a_im * lr - n_re * li) / den
    br, bi = b_re.astype(F32), b_im.astype(F32)
    bb_re = k_re[..., None] * br - k_im[..., None] * bi
    bb_im = k_re[..., None] * bi + k_im[..., None] * br
    return a_re, a_im, bb_re, bb_im


def s5_drive(ug, bb_re, bb_im):
    return (jnp.einsum('blgh,gph->blgp', ug, bb_re), jnp.einsum('blgh,gph->blgp', ug, bb_im))


def s5_readout(h_re, h_im, c_re, c_im):
    B, L = h_re.shape[:2]
    y = jnp.einsum('blgp,ghp->blgh', h_re, c_re) - jnp.einsum('blgp,ghp->blgh', h_im, c_im)
    return y.reshape(B, L, C_WIDTH)


def complex_scan(a_re, a_im, u_re, u_im, h0_re, h0_im, reverse):
    if reverse:
        u_re, u_im = jnp.flip(u_re, axis=1), jnp.flip(u_im, axis=1)
    u_re = u_re.at[:, 0].add(a_re * h0_re - a_im * h0_im)
    u_im = u_im.at[:, 0].add(a_re * h0_im + a_im * h0_re)
    ar = jnp.broadcast_to(a_re, u_re.shape)
    ai = jnp.broadcast_to(a_im, u_im.shape)

    def combine(e1, e2):
        a1r, a1i, b1r, b1i = e1
        a2r, a2i, b2r, b2i = e2
        return (a2r * a1r - a2i * a1i, a2r * a1i + a2i * a1r,
                a2r * b1r - a2i * b1i + b2r, a2r * b1i + a2i * b1r + b2i)

    _, _, h_re, h_im = lax.associative_scan(combine, (ar, ai, u_re, u_im), axis=1)
    if reverse:
        h_re, h_im = jnp.flip(h_re, axis=1), jnp.flip(h_im, axis=1)
    return h_re, h_im


def s5_glu(y, glu_w, glu_b):
    g = jax.nn.gelu(y)
    return g * jax.nn.sigmoid(g @ glu_w.astype(F32) + glu_b.astype(F32))


def s5_mixer(u_c, u_l, lam_re, lam_im, log_dt, b_re, b_im, c_re, c_im, d_skip, glu_w, glu_b, with_ctx_out):
    B = u_l.shape[0]
    uc, ul = u_c.astype(F32), u_l.astype(F32)
    ug_c = uc.reshape(B, uc.shape[1], S5_GROUPS, S5_H)
    ug_l = ul.reshape(B, ul.shape[1], S5_GROUPS, S5_H)
    dk = d_skip.astype(F32)
    y_l = dk * ul
    y_c = dk * uc if with_ctx_out else None
    h0 = jnp.zeros((B, S5_GROUPS, S5_P), F32)
    for direction in range(2):
        reverse = direction == 1
        a_re, a_im, bb_re, bb_im = s5_discretize(lam_re[direction], lam_im[direction], log_dt[direction],
                                                 b_re[direction], b_im[direction])
        cr, ci = c_re[direction].astype(F32), c_im[direction].astype(F32)
        dc_re, dc_im = s5_drive(ug_c, bb_re, bb_im)
        hc_re, hc_im = complex_scan(a_re, a_im, dc_re, dc_im, h0, h0, reverse)
        last = 0 if reverse else -1
        dl_re, dl_im = s5_drive(ug_l, bb_re, bb_im)
        hl_re, hl_im = complex_scan(a_re, a_im, dl_re, dl_im, hc_re[:, last], hc_im[:, last], reverse)
        y_l = y_l + s5_readout(hl_re, hl_im, cr, ci)
        if with_ctx_out:
            y_c = y_c + s5_readout(hc_re, hc_im, cr, ci)
    out_l = s5_glu(y_l, glu_w, glu_b).astype(u_l.dtype)
    if not with_ctx_out:
        return None, out_l
    return s5_glu(y_c, glu_w, glu_b).astype(u_c.dtype), out_l


def gmlp_mixer(u, v, ln_g, ln_b, w_s, b_s):
    u = jax.nn.gelu(u)
    v = layer_norm(jax.nn.gelu(v), ln_g, ln_b)
    B, L, _ = v.shape
    vc = v.reshape(B, L // SG_CHUNK, SG_CHUNK, SG_HEADS, SG_GROUP)
    mixed = jnp.einsum('gij,bnjgc->bnigc', w_s, vc) + b_s.T[:, :, None]
    return u * mixed.reshape(B, L, D_WIDTH)


def mixer_cd(proj, n_ctx, with_ctx_out, lam_re, lam_im, log_dt, b_re, b_im, c_re, c_im, d_skip,
             glu_w, glu_b, ln_g, ln_b, w_s, b_s):
    s, u, v = jnp.split(proj, [C_WIDTH, C_WIDTH + D_WIDTH], axis=-1)
    y_c, y_l = s5_mixer(s[:, :n_ctx], s[:, n_ctx:], lam_re, lam_im, log_dt, b_re, b_im, c_re, c_im,
                        d_skip, glu_w, glu_b, with_ctx_out)
    g_l = gmlp_mixer(u[:, n_ctx:], v[:, n_ctx:], ln_g, ln_b, w_s, b_s)
    out_l = jnp.concatenate([y_l, g_l], axis=-1)
    if not with_ctx_out:
        return None, out_l
    g_c = gmlp_mixer(u[:, :n_ctx], v[:, :n_ctx], ln_g, ln_b, w_s, b_s)
    return jnp.concatenate([y_c, g_c], axis=-1), out_l


def hier_moe(x, w_grp, b_grp, w_rt, b_rt, w_gate, w_up, w_down):
    shp = x.shape
    xt = x.reshape(-1, D_MODEL)
    N = xt.shape[0]
    grp_prob = jax.nn.softmax((xt @ w_grp).astype(F32) + b_grp.astype(F32), axis=-1)
    g_w, g_idx = lax.top_k(grp_prob, 1)
    fine = jnp.einsum('nd,gde->nge', xt, w_rt).astype(F32) + b_rt.astype(F32)
    fine = fine[jnp.arange(N), g_idx[:, 0]]
    e_val, e_idx = lax.top_k(fine, MOE_TOPK)
    e_w = jax.nn.softmax(e_val, axis=-1) * g_w
    expert_id = g_idx * MOE_PER_GROUP + e_idx
    combine = jnp.zeros((N, MOE_EXPERTS), F32).at[jnp.arange(N)[:, None], expert_id].add(e_w)
    y = jnp.zeros((N, D_MODEL), F32)
    for e in range(MOE_EXPERTS):
        h = jax.nn.silu(xt @ w_gate[e]) * (xt @ w_up[e])
        y = y + combine[:, e:e + 1] * (h @ w_down[e]).astype(F32)
    return y.astype(x.dtype).reshape(shp)


def trunk_layer(x, ctx, c, c_ctx, mod_w, mod_b, n1, n2, mixer, w_in, w_out, moe, with_ctx_out):
    sh1, sc1, g1, sh2, sc2, g2 = [t[:, None, :] for t in modulation(c, mod_w, mod_b)]
    csh1, csc1, cg1, csh2, csc2, cg2 = modulation(c_ctx, mod_w, mod_b)
    n_ctx = ctx.shape[1]
    h = jnp.concatenate([modulate(ctx, n1, csh1, csc1), modulate(x, n1, sh1, sc1)], axis=1)
    proj = h @ w_in
    o_c, o_l = mixer(proj, n_ctx, with_ctx_out)
    x = x + g1 * (o_l @ w_out)
    h2_l = modulate(x, n2, sh2, sc2)
    if with_ctx_out:
        ctx = ctx + cg1 * (o_c @ w_out)
        f = hier_moe(jnp.concatenate([modulate(ctx, n2, csh2, csc2), h2_l], axis=1), *moe)
        ctx = ctx + cg2 * f[:, :n_ctx]
        x = x + g2 * f[:, n_ctx:]
    else:
        x = x + g2 * hier_moe(h2_l, *moe)
    return x, ctx


def setup_inputs(seed: int = 0) -> dict:
    key = jax.random.key(seed)
    keys = iter(jax.random.split(key, 48))

    def nrm(shape, scale):
        return jax.random.normal(next(keys), shape, F32) * scale

    D = D_MODEL
    G, P, H = S5_GROUPS, S5_P, S5_H
    return {
        'x': nrm((BATCH, SEQ, D), 1.0),
        'c': nrm((BATCH, D), 1.0),
        'ctx': nrm((BATCH, CTX_LEN, D), 1.0),
        'c_ctx': nrm((D,), 1.0),
        'mod_w': nrm((DEPTH, D, 6 * D), 0.5 * D ** -0.5),
        'mod_b': nrm((DEPTH, 6 * D), 0.02),
        'norm1_g': 1.0 + nrm((DEPTH, D), 0.02),
        'norm2_g': 1.0 + nrm((DEPTH, D), 0.02),
        'ab_w_in': nrm((N_EVEN, D, AB_IN), D ** -0.5),
        'ab_q_norm': 1.0 + nrm((N_EVEN, HEAD_DIM), 0.02),
        'ab_k_norm': 1.0 + nrm((N_EVEN, HEAD_DIM), 0.02),
        'ab_sink': nrm((N_EVEN, A_HEADS), 1.0),
        'ab_pool_w': nrm((N_EVEN, len(POOL_WINDOWS), POOL_GROUP, POOL_GROUP), POOL_GROUP ** -0.5),
        'ab_pool_scale': 1.0 + nrm((N_EVEN, B_WIDTH), 0.02),
        'ab_w_out': nrm((N_EVEN, AB_MIX, D), AB_MIX ** -0.5),
        'cd_w_in': nrm((N_ODD, D, CD_IN), D ** -0.5),
        's5_lam_re': -0.5 + nrm((N_ODD, 2, G, P), 0.01),
        's5_lam_im': math.pi * jnp.arange(P, dtype=F32) + nrm((N_ODD, 2, G, P), 0.01),
        's5_log_dt': jax.random.uniform(next(keys), (N_ODD, 2, G), F32, math.log(DT_MIN), math.log(DT_MAX)),
        's5_b_re': nrm((N_ODD, 2, G, P, H), (2 * H) ** -0.5),
        's5_b_im': nrm((N_ODD, 2, G, P, H), (2 * H) ** -0.5),
        's5_c_re': nrm((N_ODD, 2, G, H, P), P ** -0.5),
        's5_c_im': nrm((N_ODD, 2, G, H, P), P ** -0.5),
        's5_d': nrm((N_ODD, C_WIDTH), 1.0),
        's5_glu_w': nrm((N_ODD, C_WIDTH, C_WIDTH), C_WIDTH ** -0.5),
        's5_glu_b': nrm((N_ODD, C_WIDTH), 0.02),
        'sg_ln_g': 1.0 + nrm((N_ODD, D_WIDTH), 0.02),
        'sg_ln_b': nrm((N_ODD, D_WIDTH), 0.02),
        'sg_w': nrm((N_ODD, SG_HEADS, SG_CHUNK, SG_CHUNK), 0.5 * SG_CHUNK ** -0.5),
        'sg_b': 1.0 + nrm((N_ODD, SG_HEADS, SG_CHUNK), 0.02),
        'cd_w_out': nrm((N_ODD, CD_MIX, D), CD_MIX ** -0.5),
        'moe_w_group': nrm((DEPTH, D, MOE_GROUPS), D ** -0.5),
        'moe_b_group': nrm((DEPTH, MOE_GROUPS), 0.01),
        'moe_w_router': nrm((DEPTH, MOE_GROUPS, D, MOE_PER_GROUP), D ** -0.5),
        'moe_b_router': nrm((DEPTH, MOE_GROUPS, MOE_PER_GROUP), 0.01),
        'moe_w_gate': nrm((DEPTH, MOE_EXPERTS, D, MOE_FF), D ** -0.5),
        'moe_w_up': nrm((DEPTH, MOE_EXPERTS, D, MOE_FF), D ** -0.5),
        'moe_w_down': nrm((DEPTH, MOE_EXPERTS, MOE_FF, D), MOE_FF ** -0.5),
    }


def reference(x, c, ctx, c_ctx, mod_w, mod_b, norm1_g, norm2_g,
              ab_w_in, ab_q_norm, ab_k_norm, ab_sink, ab_pool_w, ab_pool_scale, ab_w_out,
              cd_w_in, s5_lam_re, s5_lam_im, s5_log_dt, s5_b_re, s5_b_im, s5_c_re, s5_c_im, s5_d,
              s5_glu_w, s5_glu_b, sg_ln_g, sg_ln_b, sg_w, sg_b, cd_w_out,
              moe_w_group, moe_b_group, moe_w_router, moe_b_router, moe_w_gate, moe_w_up, moe_w_down):
    for layer in range(DEPTH):
        with_ctx_out = layer < DEPTH - 1
        i = layer // 2
        if layer % 2 == 0:
            mixer = functools.partial(mixer_ab, q_g=ab_q_norm[i], k_g=ab_k_norm[i], sink=ab_sink[i],
                                      w_pool=ab_pool_w[i], pool_scale=ab_pool_scale[i])
            w_in, w_out = ab_w_in[i], ab_w_out[i]
        else:
            mixer = functools.partial(mixer_cd, lam_re=s5_lam_re[i], lam_im=s5_lam_im[i], log_dt=s5_log_dt[i],
                                      b_re=s5_b_re[i], b_im=s5_b_im[i], c_re=s5_c_re[i], c_im=s5_c_im[i],
                                      d_skip=s5_d[i], glu_w=s5_glu_w[i], glu_b=s5_glu_b[i],
                                      ln_g=sg_ln_g[i], ln_b=sg_ln_b[i], w_s=sg_w[i], b_s=sg_b[i])
            w_in, w_out = cd_w_in[i], cd_w_out[i]
        moe = (moe_w_group[layer], moe_b_group[layer], moe_w_router[layer], moe_b_router[layer],
               moe_w_gate[layer], moe_w_up[layer], moe_w_down[layer])
        x, ctx = trunk_layer(x, ctx, c, c_ctx, mod_w[layer], mod_b[layer], norm1_g[layer], norm2_g[layer],
                             mixer, w_in, w_out, moe, with_ctx_out)
    return x
```

```python
import functools
import math

import jax
import jax.numpy as jnp
from jax import lax
from jax.experimental import pallas as pl
from jax.experimental.pallas import tpu as pltpu

F32 = jnp.float32
BF16 = jnp.bfloat16

EPS = 1e-6
NEG_INF = -1e30
GRID_W = 64
HEAD_DIM = 128
Q_HEADS = 16
KV_HEADS = 4
Q_PER_KV = Q_HEADS // KV_HEADS
ATT_BLOCK = 128
ROPE_BASE = 10000.0
ROPE_PAIRS = HEAD_DIM // 4
POOL_WINDOWS = (2, 4, 8, 16)
POOL_HALO = 8
S5_H = 16
S5_P = 64
S5_CHUNK = 32
SG_HEADS = 8
SG_CHUNK = 128
MOE_GROUPS = 4
MOE_PER_GROUP = 8
MOE_EXPERTS = MOE_GROUPS * MOE_PER_GROUP
ROUTER_LANES = 128
ROW_TILE = 256
MOE_TILE = 256
VMEM_LIMIT = 56 * 1024 * 1024


def _cparams(*sem):
    return pltpu.CompilerParams(dimension_semantics=sem, vmem_limit_bytes=VMEM_LIMIT)


def _sigmoid(x):
    return 1.0 / (1.0 + jnp.exp(-x))


def _gelu(x):
    return 0.5 * x * (1.0 + jnp.tanh(math.sqrt(2.0 / math.pi) * (x + 0.044715 * (x * x * x))))


def _split3(x):
    hi = x.astype(BF16)
    r1 = x - hi.astype(F32)
    mid = r1.astype(BF16)
    lo = (r1 - mid.astype(F32)).astype(BF16)
    return hi, mid, lo


def _mod_kernel(c_ref, w_ref, b_ref, o_ref):
    cv = c_ref[...]
    s = (cv * _sigmoid(cv)).astype(BF16)
    o_ref[...] = jnp.dot(s, w_ref[...].astype(BF16), preferred_element_type=F32) + b_ref[...]


def _modulation(cvecs, mod_w, mod_b):
    depth, d, n = mod_w.shape
    tn = 512
    return pl.pallas_call(
        _mod_kernel,
        out_shape=jax.ShapeDtypeStruct((depth, 8, n), F32),
        grid=(depth, n // tn),
        in_specs=[pl.BlockSpec((8, d), lambda l, j: (0, 0)),
                  pl.BlockSpec((None, d, tn), lambda l, j: (l, 0, j)),
                  pl.BlockSpec((None, 1, tn), lambda l, j: (l, 0, j))],
        out_specs=pl.BlockSpec((None, 8, tn), lambda l, j: (l, 0, j)),
        compiler_params=_cparams("parallel", "parallel"),
        name="modulation",
    )(cvecs, mod_w, mod_b.reshape(depth, 1, n))


def _modulated_norm(x, g_ref, sh_ref, sc_ref):
    ms = jnp.mean(x * x, axis=-1, keepdims=True)
    return x * lax.rsqrt(ms + EPS) * g_ref[...] * (1.0 + sc_ref[...]) + sh_ref[...]


def _norm_kernel(x_ref, g_ref, sh_ref, sc_ref, o_ref):
    o_ref[...] = _modulated_norm(x_ref[...], g_ref, sh_ref, sc_ref).astype(BF16)


def _norm_modulate(xt, g, shifts, scales):
    t, d = xt.shape
    tm = ROW_TILE
    sel = lambda i: (jnp.minimum(i, 1), 0, 0)
    return pl.pallas_call(
        _norm_kernel,
        out_shape=jax.ShapeDtypeStruct((t, d), BF16),
        grid=(t // tm,),
        in_specs=[pl.BlockSpec((tm, d), lambda i: (i, 0)),
                  pl.BlockSpec((1, d), lambda i: (0, 0)),
                  pl.BlockSpec((None, 1, d), sel),
                  pl.BlockSpec((None, 1, d), sel)],
        out_specs=pl.BlockSpec((tm, d), lambda i: (i, 0)),
        compiler_params=_cparams("parallel"),
        name="norm_modulate",
    )(xt, g.reshape(1, d), shifts, scales)


def _inproj_kernel(h_ref, w_ref, o_ref, wbf_ref):
    @pl.when(pl.program_id(1) == 0)
    def _():
        wbf_ref[...] = w_ref[...].astype(BF16)

    o_ref[...] = jnp.dot(h_ref[...], wbf_ref[...], preferred_element_type=F32)


def _in_projection(h, w):
    t, d = h.shape
    n = w.shape[1]
    tm, tn = ROW_TILE, 512
    return pl.pallas_call(
        _inproj_kernel,
        out_shape=jax.ShapeDtypeStruct((t, n), F32),
        grid=(n // tn, t // tm),
        in_specs=[pl.BlockSpec((tm, d), lambda j, i: (i, 0)),
                  pl.BlockSpec((d, tn), lambda j, i: (0, j))],
        out_specs=pl.BlockSpec((tm, tn), lambda j, i: (i, j)),
        scratch_shapes=[pltpu.VMEM((d, tn), BF16)],
        compiler_params=_cparams("arbitrary", "arbitrary"),
        name="in_projection",
    )(h, w)


def _outproj_kernel(a1_ref, a2_ref, w_ref, res_ref, gate_ref, o_ref, wbf_ref, *, k1):
    @pl.when(pl.program_id(1) == 0)
    def _():
        wbf_ref[...] = w_ref[...].astype(BF16)

    acc = jnp.dot(a1_ref[...], wbf_ref[:k1, :], preferred_element_type=F32)
    acc = acc + jnp.dot(a2_ref[...], wbf_ref[k1:, :], preferred_element_type=F32)
    o_ref[...] = res_ref[...] + gate_ref[...] * acc


def _out_projection(a1, a2, w, res, gates, res_tile_off, gate_of_tile):
    r, k1 = a1.shape
    k2 = a2.shape[1]
    d = w.shape[1]
    tm, tn = ROW_TILE, 512
    return pl.pallas_call(
        functools.partial(_outproj_kernel, k1=k1),
        out_shape=jax.ShapeDtypeStruct((r, d), F32),
        grid=(d // tn, r // tm),
        in_specs=[pl.BlockSpec((tm, k1), lambda j, i: (i, 0)),
                  pl.BlockSpec((tm, k2), lambda j, i: (i, 0)),
                  pl.BlockSpec((k1 + k2, tn), lambda j, i: (0, j)),
                  pl.BlockSpec((tm, tn), lambda j, i: (i + res_tile_off, j)),
                  pl.BlockSpec((None, 1, tn), lambda j, i: (gate_of_tile(i), 0, j))],
        out_specs=pl.BlockSpec((tm, tn), lambda j, i: (i, j)),
        scratch_shapes=[pltpu.VMEM((k1 + k2, tn), BF16)],
        compiler_params=_cparams("arbitrary", "arbitrary"),
        name="out_projection",
    )(a1, a2, w, res, gates)


def _qkv_kernel(q_ref, k_ref, v_ref, cos_ref, sa_ref, sb_ref, qg_ref, kg_ref, qo_ref, ko_ref, vo_ref):
    cos, sa, sb = cos_ref[...], sa_ref[...], sb_ref[...]

    def head(x, g, scale):
        ms = jnp.mean(x * x, axis=-1, keepdims=True)
        xn = x * lax.rsqrt(ms + EPS) * g
        r = xn * cos + pltpu.roll(xn, HEAD_DIM - ROPE_PAIRS, 1) * sa + pltpu.roll(xn, ROPE_PAIRS, 1) * sb
        return (r * scale).astype(BF16)

    for h in range(Q_HEADS):
        sl = slice(h * HEAD_DIM, (h + 1) * HEAD_DIM)
        qo_ref[:, sl] = head(q_ref[:, sl], qg_ref[...], HEAD_DIM ** -0.5)
    for h in range(KV_HEADS):
        sl = slice(h * HEAD_DIM, (h + 1) * HEAD_DIM)
        ko_ref[:, sl] = head(k_ref[:, sl], kg_ref[...], 1.0)
    vo_ref[...] = v_ref[...].astype(BF16)


def _rope_tables(n_ctx, n_lat):
    t = jnp.arange(n_lat)
    row = (t // GRID_W).astype(F32)
    col = (t % GRID_W).astype(F32)
    inv_freq = ROPE_BASE ** (-jnp.arange(ROPE_PAIRS, dtype=F32) / ROPE_PAIRS)
    ang_r = row[:, None] * inv_freq
    ang_c = col[:, None] * inv_freq
    ang = jnp.concatenate([ang_r, ang_r, ang_c, ang_c], axis=-1)
    ang = jnp.concatenate([jnp.zeros((n_ctx, HEAD_DIM), F32), ang], axis=0)
    cos, sin = jnp.cos(ang), jnp.sin(ang)
    first = (jnp.arange(HEAD_DIM) % (2 * ROPE_PAIRS)) < ROPE_PAIRS
    return cos, jnp.where(first, -sin, 0.0), jnp.where(first, 0.0, sin)


def _qkv_prepare(proj, q_g, k_g, n_ctx):
    t = proj.shape[0]
    tm = ROW_TILE
    qw, kw = Q_HEADS * HEAD_DIM, KV_HEADS * HEAD_DIM
    cos, sa, sb = _rope_tables(n_ctx, t - n_ctx)
    tab = pl.BlockSpec((tm, HEAD_DIM), lambda i: (i, 0))
    vec = pl.BlockSpec((1, HEAD_DIM), lambda i: (0, 0))
    return pl.pallas_call(
        _qkv_kernel,
        out_shape=(jax.ShapeDtypeStruct((t, qw), BF16), jax.ShapeDtypeStruct((t, kw), BF16),
                   jax.ShapeDtypeStruct((t, kw), BF16)),
        grid=(t // tm,),
        in_specs=[pl.BlockSpec((tm, qw), lambda i: (i, 0)),
                  pl.BlockSpec((tm, kw), lambda i: (i, qw // kw)),
                  pl.BlockSpec((tm, kw), lambda i: (i, qw // kw + 1)),
                  tab, tab, tab, vec, vec],
        out_specs=(pl.BlockSpec((tm, qw), lambda i: (i, 0)), pl.BlockSpec((tm, kw), lambda i: (i, 0)),
                   pl.BlockSpec((tm, kw), lambda i: (i, 0))),
        compiler_params=_cparams("parallel"),
        name="qkv_prepare",
    )(proj, proj, proj, cos, sa, sb, q_g.reshape(1, HEAD_DIM), k_g.reshape(1, HEAD_DIM))


def _attn_kernel(sink_ref, q_ref, kp_ref, kc_ref, kn_ref, kx_ref, vp_ref, vc_ref, vn_ref, vx_ref, o_ref,
                 *, ctx_blocks, n_blocks):
    b = pl.program_id(0)
    blk = ATT_BLOCK
    rows = Q_PER_KV * blk
    n_ctx = ctx_blocks * blk
    latent = b >= ctx_blocks
    lat_f = jnp.where(latent, 1.0, 0.0)
    prev_f = jnp.where(jnp.logical_and(latent, b - 1 >= ctx_blocks), 1.0, 0.0)
    next_f = jnp.where(jnp.logical_and(latent, b + 1 <= n_blocks - 1), 1.0, 0.0)
    qi = lax.broadcasted_iota(jnp.int32, (rows, 3 * blk + n_ctx), 0) % blk
    col = lax.broadcasted_iota(jnp.int32, (rows, 3 * blk + n_ctx), 1)
    kj = col % blk
    allowed = jnp.where(col < blk, jnp.where(kj >= qi, prev_f, 0.0),
                        jnp.where(col < 2 * blk, lat_f,
                                  jnp.where(col < 3 * blk, jnp.where(kj <= qi, next_f, 0.0), 1.0))) > 0.5
    for h in range(KV_HEADS):
        hs = slice(h * HEAD_DIM, (h + 1) * HEAD_DIM)
        q4 = jnp.concatenate([q_ref[:, (Q_PER_KV * h + g) * HEAD_DIM:(Q_PER_KV * h + g + 1) * HEAD_DIM]
                              for g in range(Q_PER_KV)], axis=0)
        kk = jnp.concatenate([kp_ref[:, hs], kc_ref[:, hs], kn_ref[:, hs], kx_ref[:, hs]], axis=0)
        vv = jnp.concatenate([vp_ref[:, hs], vc_ref[:, hs], vn_ref[:, hs], vx_ref[:, hs]], axis=0)
        s = lax.dot_general(q4, kk, (((1,), (1,)), ((), ())), preferred_element_type=F32)
        s = jnp.where(allowed, s, NEG_INF)
        sk = jnp.concatenate([jnp.full((blk, 1), sink_ref[Q_PER_KV * h + g], F32) for g in range(Q_PER_KV)],
                             axis=0)
        m = jnp.maximum(jnp.max(s, axis=-1, keepdims=True), sk)
        p = jnp.exp(s - m)
        den = jnp.sum(p, axis=-1, keepdims=True) + jnp.exp(sk - m)
        o = jnp.dot(p.astype(BF16), vv, preferred_element_type=F32) / den
        for g in range(Q_PER_KV):
            o_ref[:, (Q_PER_KV * h + g) * HEAD_DIM:(Q_PER_KV * h + g + 1) * HEAD_DIM] = (
                o[g * blk:(g + 1) * blk, :].astype(BF16))


def _attention(q, k, v, sink, n_ctx):
    t = q.shape[0]
    blk = ATT_BLOCK
    nb = t // blk
    cb = n_ctx // blk
    kw = k.shape[1]
    prev = pl.BlockSpec((blk, kw), lambda b, s: (jnp.maximum(b - 1, 0), 0))
    cur = pl.BlockSpec((blk, kw), lambda b, s: (b, 0))
    nxt = pl.BlockSpec((blk, kw), lambda b, s: (jnp.minimum(b + 1, nb - 1), 0))
    ctx = pl.BlockSpec((n_ctx, kw), lambda b, s: (0, 0))
    return pl.pallas_call(
        functools.partial(_attn_kernel, ctx_blocks=cb, n_blocks=nb),
        out_shape=jax.ShapeDtypeStruct(q.shape, BF16),
        grid_spec=pltpu.PrefetchScalarGridSpec(
            num_scalar_prefetch=1, grid=(nb,),
            in_specs=[pl.BlockSpec((blk, q.shape[1]), lambda b, s: (b, 0)),
                      prev, cur, nxt, ctx, prev, cur, nxt, ctx],
            out_specs=pl.BlockSpec((blk, q.shape[1]), lambda b, s: (b, 0))),
        compiler_params=_cparams("parallel"),
        name="window_attention",
    )(sink, q, k, k, k, k, v, v, v, v)


def _pool_kernel(*refs, n_ctx, n_lat, n_tiles):
    ng = len(POOL_WINDOWS)
    cur_refs, prev_refs, next_refs = refs[:ng], refs[ng:2 * ng], refs[2 * ng:3 * ng]
    w_ref, scale_ref, o_ref, xe_ref = refs[3 * ng:]
    i = pl.program_id(0)
    tm = ROW_TILE
    hl = POOL_HALO
    gw = w_ref.shape[1]
    prev_valid = i >= 2
    next_valid = jnp.logical_and(i >= 1, i < n_tiles - 1)
    pos = lax.broadcasted_iota(jnp.int32, (tm, 1), 0) + jnp.where(i == 0, 0, (i - 1) * tm)
    seg_len = jnp.where(i == 0, n_ctx, n_lat)
    for g, win in enumerate(POOL_WINDOWS):
        cur = cur_refs[g][...]
        xe_ref[0:hl, :] = jnp.where(prev_valid, prev_refs[g][...], 0.0)
        xe_ref[hl:hl + tm, :] = cur
        xe_ref[hl + tm:hl + tm + hl, :] = jnp.where(next_valid, next_refs[g][...], 0.0)
        acc = xe_ref[hl - win // 2:hl - win // 2 + tm, :]
        for o in range(1 - win // 2, win // 2):
            acc = acc + xe_ref[hl + o:hl + o + tm, :]
        cnt = jnp.minimum(pos - win // 2 + win, seg_len) - jnp.maximum(pos - win // 2, 0)
        d = acc / cnt.astype(F32) - cur
        y = jnp.dot(d.astype(BF16), w_ref[g].astype(BF16), preferred_element_type=F32)
        o_ref[:, g * gw:(g + 1) * gw] = (y * scale_ref[:, g * gw:(g + 1) * gw]).astype(BF16)


def _pool_mixer(proj, col0, w_pool, pool_scale, n_ctx):
    t = proj.shape[0]
    tm, hl = ROW_TILE, POOL_HALO
    ng, gw = w_pool.shape[0], w_pool.shape[1]
    nt = t // tm
    cb = col0 // gw
    per = tm // hl
    cur = [pl.BlockSpec((tm, gw), functools.partial(lambda i, g: (i, cb + g), g=g)) for g in range(ng)]
    prev = [pl.BlockSpec((hl, gw), functools.partial(lambda i, g: (jnp.maximum(i * per - 1, 0), cb + g), g=g))
            for g in range(ng)]
    nxt = [pl.BlockSpec((hl, gw), functools.partial(lambda i, g: (jnp.minimum((i + 1) * per, nt * per - 1), cb + g), g=g))
           for g in range(ng)]
    return pl.pallas_call(
        functools.partial(_pool_kernel, n_ctx=n_ctx, n_lat=t - n_ctx, n_tiles=nt),
        out_shape=jax.ShapeDtypeStruct((t, ng * gw), BF16),
        grid=(nt,),
        in_specs=cur + prev + nxt + [pl.BlockSpec((ng, gw, gw), lambda i: (0, 0, 0)),
                                     pl.BlockSpec((1, ng * gw), lambda i: (0, 0))],
        out_specs=pl.BlockSpec((tm, ng * gw), lambda i: (i, 0)),
        scratch_shapes=[pltpu.VMEM((tm + 2 * hl, gw), F32)],
        compiler_params=_cparams("parallel"),
        name="pool_mixer",
    )(*([proj] * (3 * ng)), w_pool, pool_scale.reshape(1, ng * gw))


def _s5_powers(lr, li, dt, kcol):
    mag = jnp.exp(kcol * (lr * dt))
    ang = kcol * (li * dt)
    return mag * jnp.cos(ang), mag * jnp.sin(ang)


def _s5_expand(z_re, z_im, e_re, e_im):
    n, p = e_re.shape
    hh = z_re.shape[0]
    er = jnp.broadcast_to(e_re[:, None, :], (n, hh, p)).reshape(n * hh, p)
    ei = jnp.broadcast_to(e_im[:, None, :], (n, hh, p)).reshape(n * hh, p)
    zr = jnp.broadcast_to(z_re[None, :, :], (n, hh, p)).reshape(n * hh, p)
    zi = jnp.broadcast_to(z_im[None, :, :], (n, hh, p)).reshape(n * hh, p)
    return zr * er - zi * ei, zr * ei + zi * er


def _s5_discretize_t(lr, li, dt, bt_re, bt_im):
    mag = jnp.exp(lr * dt)
    a_re, a_im = mag * jnp.cos(li * dt), mag * jnp.sin(li * dt)
    den = lr * lr + li * li
    n_re = a_re - 1.0
    k_re = (n_re * lr + a_im * li) / den
    k_im = (a_im * lr - n_re * li) / den
    return k_re * bt_re - k_im * bt_im, k_re * bt_im + k_im * bt_re


def _s5_group_params(lam_re_ref, lam_im_ref, ldt_ref, bt_re_ref, bt_im_ref, d):
    lr, li = lam_re_ref[d], lam_im_ref[d]
    dt = jnp.exp(ldt_ref[d])
    bb_re, bb_im = _s5_discretize_t(lr, li, dt, bt_re_ref[d], bt_im_ref[d])
    return lr, li, dt, bb_re, bb_im


def _s5_state_kernel(u_ref, lam_re_ref, lam_im_ref, ldt_ref, bt_re_ref, bt_im_ref, sf_ref, sb_ref):
    lc = S5_CHUNK
    u = u_ref[...].astype(BF16)
    t_col = lax.broadcasted_iota(jnp.int32, (lc, 1), 0).astype(F32)
    for d, out_ref, kcol in ((0, sf_ref, (lc - 1) - t_col), (1, sb_ref, t_col)):
        lr, li, dt, bb_re, bb_im = _s5_group_params(lam_re_ref, lam_im_ref, ldt_ref, bt_re_ref, bt_im_ref, d)
        e_re, e_im = _s5_powers(lr, li, dt, kcol)
        w_re, w_im = _s5_expand(bb_re, bb_im, e_re, e_im)
        w = jnp.concatenate([w_re, w_im], axis=1).astype(BF16)
        out_ref[...] = jnp.dot(u, w, preferred_element_type=F32)


def _s5_carry_kernel(sf_ref, sb_ref, lam_re_ref, lam_im_ref, ldt_ref, hf_ref, hb_ref, sw_ref, *, ctx_chunks):
    nc = sf_ref.shape[0]
    lanes = sf_ref.shape[1]
    first = (lax.broadcasted_iota(jnp.int32, (1, lanes), 1) % (2 * S5_P)) < S5_P

    def scan(d, s_ref, h_ref, order):
        lr, li, dt = lam_re_ref[d], lam_im_ref[d], jnp.exp(ldt_ref[d])
        mag = jnp.exp(S5_CHUNK * (lr * dt))
        ang = S5_CHUNK * (li * dt)
        pa = mag * jnp.cos(ang)
        a_im = mag * jnp.sin(ang)
        qa = jnp.where(first, -a_im, a_im)
        sw_ref[...] = pltpu.roll(s_ref[...], S5_P, 1)
        zero = jnp.zeros((1, lanes), F32)

        def run(lo, n, step, carry):
            def body(k, hc):
                h, hs = hc
                c = lo + k * step
                h_ref[pl.ds(c, 1), :] = h
                s = s_ref[pl.ds(c, 1), :]
                ss = sw_ref[pl.ds(c, 1), :]
                return pa * h + qa * hs + s, pa * hs - qa * h + ss
            return lax.fori_loop(0, n, body, carry)

        carry = (zero, zero)
        for lo, n, step in order:
            carry = run(lo, n, step, carry)

    scan(0, sf_ref, hf_ref, ((0, nc, 1),))
    scan(1, sb_ref, hb_ref, ((ctx_chunks - 1, ctx_chunks, -1), (nc - 1, nc - ctx_chunks, -1)))


def _s5_output_kernel(u_ref, hf_ref, hb_ref, lam_re_ref, lam_im_ref, ldt_ref, bt_re_ref, bt_im_ref,
                      c_re_ref, c_im_ref, y_ref):
    lc = S5_CHUNK
    half = lc // 2
    n = lc * S5_H
    u = u_ref[...].astype(BF16)
    t_col = lax.broadcasted_iota(jnp.int32, (lc, 1), 0).astype(F32)
    s_idx = lax.broadcasted_iota(jnp.int32, (n, n), 0) // S5_H
    t_idx = lax.broadcasted_iota(jnp.int32, (n, n), 1) // S5_H
    mt = jnp.zeros((n, n), F32)
    y = jnp.zeros((u.shape[0], n), F32)
    for d, h_ref in ((0, hf_ref), (1, hb_ref)):
        lr, li, dt, bb_re, bb_im = _s5_group_params(lam_re_ref, lam_im_ref, ldt_ref, bt_re_ref, bt_im_ref, d)
        c_re, c_im = c_re_ref[d], c_im_ref[d]
        if d == 0:
            kl, kr, kc = t_col - half, half - t_col, t_col + 1.0
            keep = s_idx <= t_idx
        else:
            kl, kr, kc = half - t_col, t_col - half, lc - t_col
            keep = s_idx >= t_idx
        l_re, l_im = _s5_expand(c_re, c_im, *_s5_powers(lr, li, dt, kl))
        r_re, r_im = _s5_expand(bb_re, bb_im, *_s5_powers(lr, li, dt, kr))
        lst = jnp.concatenate([l_re, -l_im], axis=1)
        rst = jnp.concatenate([r_re, r_im], axis=1)
        r3, l3 = _split3(rst), _split3(lst)
        m = jnp.zeros((n, n), F32)
        for ra, la in ((0, 0), (0, 1), (1, 0), (0, 2), (1, 1), (2, 0)):
            m = m + lax.dot_general(r3[ra], l3[la], (((1,), (1,)), ((), ())), preferred_element_type=F32)
        mt = mt + jnp.where(keep, m, 0.0)
        ca_re, ca_im = _s5_expand(c_re, c_im, *_s5_powers(lr, li, dt, kc))
        t2 = jnp.concatenate([ca_re, -ca_im], axis=1).astype(BF16)
        y = y + lax.dot_general(h_ref[...].astype(BF16), t2, (((1,), (1,)), ((), ())),
                                preferred_element_type=F32)
    y_ref[...] = y + jnp.dot(u, mt.astype(BF16), preferred_element_type=F32)


def _s5_scan(s_all, n_ctx, lam_re, lam_im, log_dt, b_re, b_im, c_re, c_im):
    t, cw = s_all.shape
    g, p, h, lc = cw // S5_H, S5_P, S5_H, S5_CHUNK
    nc = t // lc
    ug = s_all.reshape(nc, lc, g, h).transpose(2, 0, 1, 3).reshape(g, nc, lc * h)
    lam_re4 = lam_re.transpose(1, 0, 2)[:, :, None, :]
    lam_im4 = lam_im.transpose(1, 0, 2)[:, :, None, :]
    ldt4 = jnp.broadcast_to(log_dt.T[:, :, None, None], (g, 2, 1, p))
    bt_re = b_re.transpose(1, 0, 3, 2)
    bt_im = b_im.transpose(1, 0, 3, 2)
    c_re4 = c_re.transpose(1, 0, 2, 3)
    c_im4 = c_im.transpose(1, 0, 2, 3)
    vec4 = pl.BlockSpec((None, 2, 1, p), lambda i: (i, 0, 0, 0))
    mat4 = pl.BlockSpec((None, 2, h, p), lambda i: (i, 0, 0, 0))
    u_spec = pl.BlockSpec((None, nc, lc * h), lambda i: (i, 0, 0))
    st_spec = pl.BlockSpec((nc, 2 * p), lambda i: (0, i))
    st_shape = jax.ShapeDtypeStruct((nc, g * 2 * p), F32)
    s_f, s_b = pl.pallas_call(
        _s5_state_kernel, out_shape=(st_shape, st_shape), grid=(g,),
        in_specs=[u_spec, vec4, vec4, vec4, mat4, mat4],
        out_specs=(st_spec, st_spec),
        compiler_params=_cparams("parallel"), name="s5_chunk_states",
    )(ug, lam_re4, lam_im4, ldt4, bt_re, bt_im)

    def lanes(x):
        return jnp.concatenate([x, x], axis=-1).reshape(2, 1, g * 2 * p)
    cols = 8 * 2 * p
    row = pl.BlockSpec((2, 1, cols), lambda j: (0, 0, j))
    blk = pl.BlockSpec((nc, cols), lambda j: (0, j))
    h_f, h_b = pl.pallas_call(
        functools.partial(_s5_carry_kernel, ctx_chunks=n_ctx // lc),
        out_shape=(st_shape, st_shape), grid=(g * 2 * p // cols,),
        in_specs=[blk, blk, row, row, row], out_specs=(blk, blk),
        scratch_shapes=[pltpu.VMEM((nc, cols), F32)],
        compiler_params=_cparams("parallel"), name="s5_chunk_carry",
    )(s_f, s_b, lanes(lam_re), lanes(lam_im), lanes(jnp.broadcast_to(log_dt[:, :, None], (2, g, p))))

    yg = pl.pallas_call(
        _s5_output_kernel, out_shape=jax.ShapeDtypeStruct((g, nc, lc * h), F32), grid=(g,),
        in_specs=[u_spec, st_spec, st_spec, vec4, vec4, vec4, mat4, mat4, mat4, mat4],
        out_specs=u_spec,
        compiler_params=_cparams("parallel"), name="s5_outputs",
    )(ug, h_f, h_b, lam_re4, lam_im4, ldt4, bt_re, bt_im, c_re4, c_im4)
    return yg.reshape(g, nc, lc, h).transpose(1, 2, 0, 3).reshape(t, cw)


def _glu_kernel(y_ref, u_ref, dk_ref, w_ref, b_ref, o_ref, wbf_ref):
    @pl.when(pl.program_id(0) == 0)
    def _():
        wbf_ref[...] = w_ref[...].astype(BF16)

    yl = dk_ref[...] * u_ref[...] + y_ref[...]
    gl = _gelu(yl)
    z = jnp.dot(gl.astype(BF16), wbf_ref[...], preferred_element_type=F32) + b_ref[...]
    o_ref[...] = (gl * _sigmoid(z)).astype(BF16)


def _s5_glu(y_scan, proj, d_skip, glu_w, glu_b, n_ctx):
    t, cw = y_scan.shape
    tm = ROW_TILE
    off = n_ctx // tm
    n_lat = t - n_ctx
    vec = pl.BlockSpec((1, cw), lambda i: (0, 0))
    return pl.pallas_call(
        _glu_kernel, out_shape=jax.ShapeDtypeStruct((n_lat, cw), BF16), grid=(n_lat // tm,),
        in_specs=[pl.BlockSpec((tm, cw), lambda i: (i + off, 0)),
                  pl.BlockSpec((tm, cw), lambda i: (i + off, 0)),
                  vec, pl.BlockSpec((cw, cw), lambda i: (0, 0)), vec],
        out_specs=pl.BlockSpec((tm, cw), lambda i: (i, 0)),
        scratch_shapes=[pltpu.VMEM((cw, cw), BF16)],
        compiler_params=_cparams("arbitrary"), name="s5_glu",
    )(y_scan, proj, d_skip.reshape(1, cw), glu_w, glu_b.reshape(1, cw))


def _gmlp_kernel(*refs, n_col):
    u_refs, v_refs = refs[:n_col], refs[n_col:2 * n_col]
    lng_ref, lnb_ref, ws_ref, bst_ref, o_ref = refs[2 * n_col:]
    u = _gelu(jnp.concatenate([r[...] for r in u_refs], axis=1))
    v = _gelu(jnp.concatenate([r[...] for r in v_refs], axis=1))
    mu = jnp.mean(v, axis=-1, keepdims=True)
    vc = v - mu
    var = jnp.mean(vc * vc, axis=-1, keepdims=True)
    vn = (vc * lax.rsqrt(var + EPS) * lng_ref[...] + lnb_ref[...]).astype(BF16)
    gw = u.shape[1] // SG_HEADS
    for c in range(u.shape[0] // SG_CHUNK):
        rs = slice(c * SG_CHUNK, (c + 1) * SG_CHUNK)
        for g in range(SG_HEADS):
            cs = slice(g * gw, (g + 1) * gw)
            mixed = jnp.dot(ws_ref[g].astype(BF16), vn[rs, cs], preferred_element_type=F32)
            mixed = mixed + bst_ref[:, g:g + 1]
            o_ref[rs, cs] = (u[rs, cs] * mixed).astype(BF16)


def _gmlp_mixer(proj, col_u, col_v, width, ln_g, ln_b, w_s, b_s, n_ctx):
    t = proj.shape[0]
    tm, cwid = ROW_TILE, 1024
    off = n_ctx // tm
    n_lat = t - n_ctx
    n_col = width // cwid
    specs = [pl.BlockSpec((tm, cwid), functools.partial(lambda i, cb: (i + off, cb), cb=(c0 // cwid) + k))
             for c0 in (col_u, col_v) for k in range(n_col)]
    vec = pl.BlockSpec((1, width), lambda i: (0, 0))
    return pl.pallas_call(
        functools.partial(_gmlp_kernel, n_col=n_col),
        out_shape=jax.ShapeDtypeStruct((n_lat, width), BF16), grid=(n_lat // tm,),
        in_specs=specs + [vec, vec, pl.BlockSpec(w_s.shape, lambda i: (0, 0, 0)),
                          pl.BlockSpec((SG_CHUNK, SG_HEADS), lambda i: (0, 0))],
        out_specs=pl.BlockSpec((tm, width), lambda i: (i, 0)),
        compiler_params=_cparams("parallel"), name="gmlp_mixer",
    )(*([proj] * (2 * n_col)), ln_g.reshape(1, width), ln_b.reshape(1, width), w_s, b_s.T)


def _router_kernel(x_ref, g_ref, sh_ref, sc_ref, wr_ref, br_ref, hp_ref, rt_ref, w3_ref):
    @pl.when(pl.program_id(0) == 0)
    def _():
        hi, mid, lo = _split3(wr_ref[...])
        w3_ref[0], w3_ref[1], w3_ref[2] = hi, mid, lo

    h = _modulated_norm(x_ref[...], g_ref, sh_ref, sc_ref)
    half = h.shape[1] // 2
    hp_ref[...] = pltpu.pack_elementwise([h[:, :half], h[:, half:]], packed_dtype=jnp.bfloat16)

    h3 = _split3(h)
    logits = br_ref[...]
    for a, b in ((0, 0), (0, 1), (1, 0), (0, 2), (1, 1), (2, 0)):
        logits = logits + jnp.dot(h3[a], w3_ref[b], preferred_element_type=F32)

    lane = lax.broadcasted_iota(jnp.int32, logits.shape, 1)
    big = jnp.int32(ROUTER_LANES)
    gl = jnp.where(lane < MOE_GROUPS, logits, NEG_INF)
    gm = jnp.max(gl, axis=-1, keepdims=True)
    g_w = 1.0 / jnp.sum(jnp.exp(gl - gm), axis=-1, keepdims=True)
    g_idx = jnp.min(jnp.where(gl == gm, lane, big), axis=-1, keepdims=True)
    lo_lane = MOE_GROUPS + g_idx * MOE_PER_GROUP
    fl = jnp.where(jnp.logical_and(lane >= lo_lane, lane < lo_lane + MOE_PER_GROUP), logits, NEG_INF)
    m1 = jnp.max(fl, axis=-1, keepdims=True)
    i1 = jnp.min(jnp.where(fl == m1, lane, big), axis=-1, keepdims=True)
    fl2 = jnp.where(lane == i1, NEG_INF, fl)
    m2 = jnp.max(fl2, axis=-1, keepdims=True)
    i2 = jnp.min(jnp.where(fl2 == m2, lane, big), axis=-1, keepdims=True)
    e2 = jnp.exp(m2 - m1)
    w1 = g_w / (1.0 + e2)
    w2 = g_w * e2 / (1.0 + e2)
    rt_ref[...] = jnp.where(lane == 0, (i1 - MOE_GROUPS).astype(F32),
                            jnp.where(lane == 1, (i2 - MOE_GROUPS).astype(F32),
                                      jnp.where(lane == 2, w1, jnp.where(lane == 3, w2, 0.0))))


def _moe_router(xt, tile_off, n_rows, g, shifts, scales, first_latent_tile, w_grp, b_grp, w_rt, b_rt):
    d = xt.shape[1]
    tm = ROW_TILE
    ne = MOE_GROUPS * MOE_PER_GROUP
    wr = jnp.concatenate([w_grp, w_rt.transpose(1, 0, 2).reshape(d, ne),
                          jnp.zeros((d, ROUTER_LANES - MOE_GROUPS - ne), F32)], axis=1)
    br = jnp.concatenate([b_grp, b_rt.reshape(ne), jnp.zeros((ROUTER_LANES - MOE_GROUPS - ne,), F32)])
    sel = lambda i: (jnp.where(i + tile_off >= first_latent_tile, 1, 0), 0, 0)
    return pl.pallas_call(
        _router_kernel,
        out_shape=(jax.ShapeDtypeStruct((n_rows, d // 2), jnp.int32),
                   jax.ShapeDtypeStruct((n_rows, ROUTER_LANES), F32)),
        grid=(n_rows // tm,),
        in_specs=[pl.BlockSpec((tm, d), lambda i: (i + tile_off, 0)),
                  pl.BlockSpec((1, d), lambda i: (0, 0)),
                  pl.BlockSpec((None, 1, d), sel), pl.BlockSpec((None, 1, d), sel),
                  pl.BlockSpec((d, ROUTER_LANES), lambda i: (0, 0)),
                  pl.BlockSpec((1, ROUTER_LANES), lambda i: (0, 0))],
        out_specs=(pl.BlockSpec((tm, d // 2), lambda i: (i, 0)),
                   pl.BlockSpec((tm, ROUTER_LANES), lambda i: (i, 0))),
        scratch_shapes=[pltpu.VMEM((3, d, ROUTER_LANES), BF16)],
        compiler_params=_cparams("arbitrary"), name="moe_router",
    )(xt, g.reshape(1, d), shifts, scales, wr, br.reshape(1, ROUTER_LANES))


def _row_gather_kernel(nu_ref, idx_ref, src_hbm, out_ref, sem):
    t = pl.program_id(0)
    tm = out_ref.shape[0]

    @pl.when(t < nu_ref[0])
    def _():
        def issue(r, carry):
            tok = idx_ref[t * tm + r]
            pltpu.make_async_copy(src_hbm.at[pl.ds(tok, 1)], out_ref.at[pl.ds(r, 1)], sem).start()
            return carry

        lax.fori_loop(0, tm, issue, 0)
        pltpu.make_async_copy(src_hbm.at[pl.ds(0, tm)], out_ref, sem).wait()


def _moe_dispatch(hp, src_tok, n_used, n_tiles):
    tm = MOE_TILE
    w = hp.shape[1]
    last = lambda t, nu, idx: (jnp.minimum(t, nu[0] - 1), 0)
    return pl.pallas_call(
        _row_gather_kernel,
        out_shape=jax.ShapeDtypeStruct((n_tiles * tm, w), hp.dtype),
        grid_spec=pltpu.PrefetchScalarGridSpec(
            num_scalar_prefetch=2, grid=(n_tiles,),
            in_specs=[pl.BlockSpec(memory_space=pl.ANY)],
            out_specs=pl.BlockSpec((tm, w), last),
            scratch_shapes=[pltpu.SemaphoreType.DMA(())]),
        compiler_params=_cparams("arbitrary"), name="moe_dispatch",
    )(n_used, src_tok, hp)


def _unpack_pairs(xp):
    lo = pltpu.unpack_elementwise(xp, index=0, packed_dtype=jnp.bfloat16, unpacked_dtype=F32)
    hi = pltpu.unpack_elementwise(xp, index=1, packed_dtype=jnp.bfloat16, unpacked_dtype=F32)
    return lo.astype(BF16), hi.astype(BF16)


def _expert_up_kernel(te_ref, nu_ref, xs_ref, wg_ref, wu_ref, h_ref, wgb_ref, wub_ref):
    t = pl.program_id(1)
    tc = jnp.minimum(t, nu_ref[0] - 1)
    new_expert = jnp.logical_or(t == 0, te_ref[tc] != te_ref[jnp.maximum(tc - 1, 0)])

    @pl.when(t < nu_ref[0])
    def _():
        @pl.when(new_expert)
        def _():
            wgb_ref[...] = wg_ref[...].astype(BF16)
            wub_ref[...] = wu_ref[...].astype(BF16)

        lo, hi = _unpack_pairs(xs_ref[...])
        half = lo.shape[1]
        gate = (jnp.dot(lo, wgb_ref[:half, :], preferred_element_type=F32)
                + jnp.dot(hi, wgb_ref[half:, :], preferred_element_type=F32))
        up = (jnp.dot(lo, wub_ref[:half, :], preferred_element_type=F32)
              + jnp.dot(hi, wub_ref[half:, :], preferred_element_type=F32))
        h_ref[...] = (gate * _sigmoid(gate) * up).astype(BF16)


def _expert_down_kernel(te_ref, nu_ref, h_ref, rw_ref, wd_ref, y_ref, wdb_ref):
    t = pl.program_id(1)
    tc = jnp.minimum(t, nu_ref[0] - 1)
    new_expert = jnp.logical_or(t == 0, te_ref[tc] != te_ref[jnp.maximum(tc - 1, 0)])

    @pl.when(t < nu_ref[0])
    def _():
        @pl.when(new_expert)
        def _():
            wdb_ref[...] = wd_ref[...].astype(BF16)

        y_ref[...] = jnp.dot(h_ref[...], wdb_ref[...], preferred_element_type=F32) * rw_ref[...]


def _moe_experts(xs, row_w, tile_expert, n_used, w_gate, w_up, w_down):
    tm = MOE_TILE
    npad, half = xs.shape
    d = 2 * half
    ff = w_gate.shape[2]
    nt = npad // tm
    fh, dh = ff // 2, d // 2
    tile = lambda t, nu: jnp.minimum(t, nu[0] - 1)
    hmid = pl.pallas_call(
        _expert_up_kernel,
        out_shape=jax.ShapeDtypeStruct((npad, ff), BF16),
        grid_spec=pltpu.PrefetchScalarGridSpec(
            num_scalar_prefetch=2, grid=(ff // fh, nt),
            in_specs=[pl.BlockSpec((tm, half), lambda j, t, te, nu: (tile(t, nu), 0)),
                      pl.BlockSpec((None, d, fh), lambda j, t, te, nu: (te[tile(t, nu)], 0, j)),
                      pl.BlockSpec((None, d, fh), lambda j, t, te, nu: (te[tile(t, nu)], 0, j))],
            out_specs=pl.BlockSpec((tm, fh), lambda j, t, te, nu: (tile(t, nu), j)),
            scratch_shapes=[pltpu.VMEM((d, fh), BF16), pltpu.VMEM((d, fh), BF16)]),
        compiler_params=_cparams("arbitrary", "arbitrary"), name="moe_expert_up",
    )(tile_expert, n_used, xs, w_gate, w_up)
    return pl.pallas_call(
        _expert_down_kernel,
        out_shape=jax.ShapeDtypeStruct((npad, d), F32),
        grid_spec=pltpu.PrefetchScalarGridSpec(
            num_scalar_prefetch=2, grid=(d // dh, nt),
            in_specs=[pl.BlockSpec((tm, ff), lambda j, t, te, nu: (tile(t, nu), 0)),
                      pl.BlockSpec((tm, 1), lambda j, t, te, nu: (tile(t, nu), 0)),
                      pl.BlockSpec((None, ff, dh), lambda j, t, te, nu: (te[tile(t, nu)], 0, j))],
            out_specs=pl.BlockSpec((tm, dh), lambda j, t, te, nu: (tile(t, nu), j)),
            scratch_shapes=[pltpu.VMEM((ff, dh), BF16)]),
        compiler_params=_cparams("arbitrary", "arbitrary"), name="moe_expert_down",
    )(tile_expert, n_used, hmid, row_w, w_down)


def _combine_kernel(pos_ref, y_hbm, res_ref, gate_ref, o_ref, buf_ref, sem):
    i = pl.program_id(0)
    tm = o_ref.shape[0]

    def issue(r, carry):
        for k in range(2):
            p = pos_ref[2 * (i * tm + r) + k]
            pltpu.make_async_copy(y_hbm.at[pl.ds(p, 1)], buf_ref.at[k, pl.ds(r, 1)], sem).start()
        return carry

    lax.fori_loop(0, tm, issue, 0)
    pltpu.make_async_copy(y_hbm.at[pl.ds(0, 2 * tm)], buf_ref.reshape(2 * tm, buf_ref.shape[2]), sem).wait()
    o_ref[...] = res_ref[...] + gate_ref[...] * (buf_ref[0] + buf_ref[1])


def _moe_combine(y_sorted, pos, res, res_tile_off, n_rows, gates, first_latent_tile):
    d = y_sorted.shape[1]
    tm = ROW_TILE
    sel = lambda i, p: (jnp.where(i + res_tile_off >= first_latent_tile, 1, 0), 0, 0)
    return pl.pallas_call(
        _combine_kernel,
        out_shape=jax.ShapeDtypeStruct((n_rows, d), F32),
        grid_spec=pltpu.PrefetchScalarGridSpec(
            num_scalar_prefetch=1, grid=(n_rows // tm,),
            in_specs=[pl.BlockSpec(memory_space=pl.ANY),
                      pl.BlockSpec((tm, d), lambda i, p: (i + res_tile_off, 0)),
                      pl.BlockSpec((None, 1, d), sel)],
            out_specs=pl.BlockSpec((tm, d), lambda i, p: (i, 0)),
            scratch_shapes=[pltpu.VMEM((2, tm, d), F32), pltpu.SemaphoreType.DMA(())]),
        compiler_params=_cparams("arbitrary"), name="moe_combine",
    )(pos.reshape(-1), y_sorted, res, gates)


def _routing_plan(route, n_tiles):
    tm = MOE_TILE
    n = route.shape[0]
    ids = route[:, 0:2].astype(jnp.int32).reshape(-1)
    wts = route[:, 2:4].reshape(-1)
    onehot = (ids[:, None] == jnp.arange(MOE_EXPERTS)[None, :]).astype(jnp.int32)
    rank = jnp.take_along_axis(jnp.cumsum(onehot, axis=0), ids[:, None], axis=1)[:, 0] - 1
    counts = jnp.sum(onehot, axis=0)
    tiles = (counts + tm - 1) // tm
    tile_end = jnp.cumsum(tiles)
    tile_start = tile_end - tiles
    pos = tile_start[ids] * tm + rank
    tok = jnp.arange(2 * n, dtype=jnp.int32) // 2
    src_tok = jnp.zeros((n_tiles * tm,), jnp.int32).at[pos].set(tok)
    row_w = jnp.zeros((n_tiles * tm,), F32).at[pos].set(wts)
    n_used = tile_end[-1].astype(jnp.int32).reshape(1)
    tile_expert = jnp.minimum(jnp.searchsorted(tile_end, jnp.arange(n_tiles), side="right"),
                              MOE_EXPERTS - 1).astype(jnp.int32)
    return pos.astype(jnp.int32), src_tok, row_w.reshape(-1, 1), tile_expert, n_used


def _hier_moe(xt, tile_off, n_rows, norm_g, shifts, scales, gates, first_latent_tile, moe):
    w_grp, b_grp, w_rt, b_rt, w_gate, w_up, w_down = moe
    hp, route = _moe_router(xt, tile_off, n_rows, norm_g, shifts, scales, first_latent_tile,
                            w_grp, b_grp, w_rt, b_rt)
    n_tiles = (2 * n_rows) // MOE_TILE + MOE_EXPERTS
    pos, src_tok, row_w, tile_expert, n_used = _routing_plan(route, n_tiles)
    xs = _moe_dispatch(hp, src_tok, n_used, n_tiles)
    y_sorted = _moe_experts(xs, row_w, tile_expert, n_used, w_gate, w_up, w_down)
    return _moe_combine(y_sorted, pos, xt, tile_off, n_rows, gates, first_latent_tile)


def kernel(x, c, ctx, c_ctx, mod_w, mod_b, norm1_g, norm2_g, ab_w_in, ab_q_norm, ab_k_norm, ab_sink, ab_pool_w, ab_pool_scale, ab_w_out, cd_w_in, s5_lam_re, s5_lam_im, s5_log_dt, s5_b_re, s5_b_im, s5_c_re, s5_c_im, s5_d, s5_glu_w, s5_glu_b, sg_ln_g, sg_ln_b, sg_w, sg_b, cd_w_out, moe_w_group, moe_b_group, moe_w_router, moe_b_router, moe_w_gate, moe_w_up, moe_w_down):
    batch, n_lat, d = x.shape
    n_ctx = ctx.shape[1]
    assert batch == 1 and n_ctx == ROW_TILE and n_lat % ROW_TILE == 0 and mod_w.shape[0] == 2
    ctx_tiles = n_ctx // ROW_TILE

    cvecs = jnp.concatenate([c_ctx[None, :], c, jnp.zeros((6, d), F32)], axis=0)
    mod = _modulation(cvecs, mod_w, mod_b)[:, 0:2, :].reshape(2, 2, 6, 1, d)
    xt = jnp.concatenate([ctx[0], x[0]], axis=0)

    def moe_params(layer):
        return (moe_w_group[layer], moe_b_group[layer], moe_w_router[layer], moe_b_router[layer],
                moe_w_gate[layer], moe_w_up[layer], moe_w_down[layer])

    m0 = mod[0]
    sh1, sc1, g1, sh2, sc2, g2 = (m0[:, k] for k in range(6))
    h = _norm_modulate(xt, norm1_g[0], sh1, sc1)
    proj = _in_projection(h, ab_w_in[0])
    q, k, v = _qkv_prepare(proj, ab_q_norm[0], ab_k_norm[0], n_ctx)
    att = _attention(q, k, v, ab_sink[0], n_ctx)
    pooled = _pool_mixer(proj, Q_HEADS * HEAD_DIM + 2 * KV_HEADS * HEAD_DIM, ab_pool_w[0], ab_pool_scale[0], n_ctx)
    xt = _out_projection(att, pooled, ab_w_out[0], xt, g1, 0, lambda i: jnp.minimum(i, 1))
    xt = _hier_moe(xt, 0, n_ctx + n_lat, norm2_g[0], sh2, sc2, g2, ctx_tiles, moe_params(0))

    m1 = mod[1]
    sh1, sc1, g1, sh2, sc2, g2 = (m1[:, k] for k in range(6))
    h = _norm_modulate(xt, norm1_g[1], sh1, sc1)
    proj = _in_projection(h, cd_w_in[0])
    cw = s5_d.shape[1]
    dw = sg_ln_g.shape[1]
    y_scan = _s5_scan(proj[:, :cw], n_ctx, s5_lam_re[0], s5_lam_im[0], s5_log_dt[0], s5_b_re[0], s5_b_im[0],
                      s5_c_re[0], s5_c_im[0])
    y_c = _s5_glu(y_scan, proj, s5_d[0], s5_glu_w[0], s5_glu_b[0], n_ctx)
    y_d = _gmlp_mixer(proj, cw, cw + dw, dw, sg_ln_g[0], sg_ln_b[0], sg_w[0], sg_b[0], n_ctx)
    xl = _out_projection(y_c, y_d, cd_w_out[0], xt, g1, ctx_tiles, lambda i: 1)
    out = _hier_moe(xl, 0, n_lat, norm2_g[1], sh2, sc2, g2, 0, moe_params(1))
    return out[None]
```

```python
import functools
import math

import jax
import jax.numpy as jnp
from jax import lax
from jax.experimental import pallas as pl
from jax.experimental.pallas import tpu as pltpu

F32 = jnp.float32
BF16 = jnp.bfloat16

EPS = 1e-6
NEG_INF = -1e30
GRID_W = 64
HEAD_DIM = 128
Q_HEADS = 16
KV_HEADS = 4
Q_PER_KV = Q_HEADS // KV_HEADS
ATT_BLOCK = 128
ROPE_BASE = 10000.0
ROPE_PAIRS = HEAD_DIM // 4
POOL_WINDOWS = (2, 4, 8, 16)
POOL_HALO = 8
S5_H = 16
S5_P = 64
S5_CHUNK = 32
SG_HEADS = 8
SG_CHUNK = 128
MOE_GROUPS = 4
MOE_PER_GROUP = 8
MOE_EXPERTS = MOE_GROUPS * MOE_PER_GROUP
ROUTER_LANES = 128
ROW_TILE = 256
MOE_TILE = 256
VMEM_LIMIT = 56 * 1024 * 1024


DMA_ISSUE_UNROLL = 8


def _cparams(*sem, unchecked_dma=False):
    return pltpu.CompilerParams(dimension_semantics=sem, vmem_limit_bytes=VMEM_LIMIT,
                                disable_bounds_checks=unchecked_dma)


def _sigmoid(x):
    return 1.0 / (1.0 + jnp.exp(-x))


def _gelu(x):
    return 0.5 * x * (1.0 + jnp.tanh(math.sqrt(2.0 / math.pi) * (x + 0.044715 * (x * x * x))))


def _split3(x):
    hi = x.astype(BF16)
    r1 = x - hi.astype(F32)
    mid = r1.astype(BF16)
    lo = (r1 - mid.astype(F32)).astype(BF16)
    return hi, mid, lo


def _mod_kernel(c_ref, w_ref, b_ref, o_ref):
    cv = c_ref[...]
    s = (cv * _sigmoid(cv)).astype(BF16)
    o_ref[...] = jnp.dot(s, w_ref[...].astype(BF16), preferred_element_type=F32) + b_ref[...]


def _modulation(cvecs, mod_w, mod_b):
    depth, d, n = mod_w.shape
    tn = 512
    return pl.pallas_call(
        _mod_kernel,
        out_shape=jax.ShapeDtypeStruct((depth, 8, n), F32),
        grid=(depth, n // tn),
        in_specs=[pl.BlockSpec((8, d), lambda l, j: (0, 0)),
                  pl.BlockSpec((None, d, tn), lambda l, j: (l, 0, j)),
                  pl.BlockSpec((None, 1, tn), lambda l, j: (l, 0, j))],
        out_specs=pl.BlockSpec((None, 8, tn), lambda l, j: (l, 0, j)),
        compiler_params=_cparams("parallel", "parallel"),
        name="modulation",
    )(cvecs, mod_w, mod_b.reshape(depth, 1, n))


def _modulated_norm(x, g_ref, sh_ref, sc_ref):
    ms = jnp.mean(x * x, axis=-1, keepdims=True)
    return x * lax.rsqrt(ms + EPS) * g_ref[...] * (1.0 + sc_ref[...]) + sh_ref[...]


def _norm_kernel(x_ref, g_ref, sh_ref, sc_ref, o_ref):
    o_ref[...] = _modulated_norm(x_ref[...], g_ref, sh_ref, sc_ref).astype(BF16)


def _norm_modulate(xt, g, shifts, scales):
    t, d = xt.shape
    tm = ROW_TILE
    sel = lambda i: (jnp.minimum(i, 1), 0, 0)
    return pl.pallas_call(
        _norm_kernel,
        out_shape=jax.ShapeDtypeStruct((t, d), BF16),
        grid=(t // tm,),
        in_specs=[pl.BlockSpec((tm, d), lambda i: (i, 0)),
                  pl.BlockSpec((1, d), lambda i: (0, 0)),
                  pl.BlockSpec((None, 1, d), sel),
                  pl.BlockSpec((None, 1, d), sel)],
        out_specs=pl.BlockSpec((tm, d), lambda i: (i, 0)),
        compiler_params=_cparams("parallel"),
        name="norm_modulate",
    )(xt, g.reshape(1, d), shifts, scales)


def _inproj_kernel(h_ref, w_ref, o_ref, wbf_ref):
    @pl.when(pl.program_id(1) == 0)
    def _():
        wbf_ref[...] = w_ref[...].astype(BF16)

    o_ref[...] = jnp.dot(h_ref[...], wbf_ref[...], preferred_element_type=F32)


def _in_projection(h, w):
    t, d = h.shape
    n = w.shape[1]
    tm = 3 * ROW_TILE if t % (3 * ROW_TILE) == 0 else ROW_TILE
    tn = 512
    return pl.pallas_call(
        _inproj_kernel,
        out_shape=jax.ShapeDtypeStruct((t, n), F32),
        grid=(n // tn, t // tm),
        in_specs=[pl.BlockSpec((tm, d), lambda j, i: (i, 0)),
                  pl.BlockSpec((d, tn), lambda j, i: (0, j))],
        out_specs=pl.BlockSpec((tm, tn), lambda j, i: (i, j)),
        scratch_shapes=[pltpu.VMEM((d, tn), BF16)],
        compiler_params=_cparams("arbitrary", "arbitrary"),
        name="in_projection",
    )(h, w)


def _outproj_kernel(a1_ref, a2_ref, w_ref, res_ref, gate_ref, o_ref, wbf_ref, *, k1):
    @pl.when(pl.program_id(1) == 0)
    def _():
        wbf_ref[...] = w_ref[...].astype(BF16)

    acc = jnp.dot(a1_ref[...], wbf_ref[:k1, :], preferred_element_type=F32)
    acc = acc + jnp.dot(a2_ref[...], wbf_ref[k1:, :], preferred_element_type=F32)
    o_ref[...] = res_ref[...] + gate_ref[...] * acc


def _out_projection(a1, a2, w, res, gates, res_tile_off, gate_of_tile):
    r, k1 = a1.shape
    k2 = a2.shape[1]
    d = w.shape[1]
    tm, tn = ROW_TILE, 1024
    return pl.pallas_call(
        functools.partial(_outproj_kernel, k1=k1),
        out_shape=jax.ShapeDtypeStruct((r, d), F32),
        grid=(d // tn, r // tm),
        in_specs=[pl.BlockSpec((tm, k1), lambda j, i: (i, 0)),
                  pl.BlockSpec((tm, k2), lambda j, i: (i, 0)),
                  pl.BlockSpec((k1 + k2, tn), lambda j, i: (0, j)),
                  pl.BlockSpec((tm, tn), lambda j, i: (i + res_tile_off, j)),
                  pl.BlockSpec((None, 1, tn), lambda j, i: (gate_of_tile(i), 0, j))],
        out_specs=pl.BlockSpec((tm, tn), lambda j, i: (i, j)),
        scratch_shapes=[pltpu.VMEM((k1 + k2, tn), BF16)],
        compiler_params=_cparams("arbitrary", "arbitrary"),
        name="out_projection",
    )(a1, a2, w, res, gates)


def _qkv_kernel(q_ref, k_ref, v_ref, cos_ref, sa_ref, sb_ref, qg_ref, kg_ref, qo_ref, ko_ref, vo_ref):
    cos, sa, sb = cos_ref[...], sa_ref[...], sb_ref[...]

    def head(x, g, scale):
        ms = jnp.mean(x * x, axis=-1, keepdims=True)
        xn = x * lax.rsqrt(ms + EPS) * g
        r = xn * cos + pltpu.roll(xn, HEAD_DIM - ROPE_PAIRS, 1) * sa + pltpu.roll(xn, ROPE_PAIRS, 1) * sb
        return (r * scale).astype(BF16)

    for h in range(Q_HEADS):
        sl = slice(h * HEAD_DIM, (h + 1) * HEAD_DIM)
        qo_ref[:, sl] = head(q_ref[:, sl], qg_ref[...], HEAD_DIM ** -0.5)
    for h in range(KV_HEADS):
        sl = slice(h * HEAD_DIM, (h + 1) * HEAD_DIM)
        ko_ref[:, sl] = head(k_ref[:, sl], kg_ref[...], 1.0)
    vo_ref[...] = v_ref[...].astype(BF16)


def _rope_tables(n_ctx, n_lat):
    t = jnp.arange(n_lat)
    row = (t // GRID_W).astype(F32)
    col = (t % GRID_W).astype(F32)
    inv_freq = ROPE_BASE ** (-jnp.arange(ROPE_PAIRS, dtype=F32) / ROPE_PAIRS)
    ang_r = row[:, None] * inv_freq
    ang_c = col[:, None] * inv_freq
    ang = jnp.concatenate([ang_r, ang_r, ang_c, ang_c], axis=-1)
    ang = jnp.concatenate([jnp.zeros((n_ctx, HEAD_DIM), F32), ang], axis=0)
    cos, sin = jnp.cos(ang), jnp.sin(ang)
    first = (jnp.arange(HEAD_DIM) % (2 * ROPE_PAIRS)) < ROPE_PAIRS
    return cos, jnp.where(first, -sin, 0.0), jnp.where(first, 0.0, sin)


def _qkv_prepare(proj, q_g, k_g, n_ctx):
    t = proj.shape[0]
    tm = ROW_TILE
    qw, kw = Q_HEADS * HEAD_DIM, KV_HEADS * HEAD_DIM
    cos, sa, sb = _rope_tables(n_ctx, t - n_ctx)
    tab = pl.BlockSpec((tm, HEAD_DIM), lambda i: (i, 0))
    vec = pl.BlockSpec((1, HEAD_DIM), lambda i: (0, 0))
    return pl.pallas_call(
        _qkv_kernel,
        out_shape=(jax.ShapeDtypeStruct((t, qw), BF16), jax.ShapeDtypeStruct((t, kw), BF16),
                   jax.ShapeDtypeStruct((t, kw), BF16)),
        grid=(t // tm,),
        in_specs=[pl.BlockSpec((tm, qw), lambda i: (i, 0)),
                  pl.BlockSpec((tm, kw), lambda i: (i, qw // kw)),
                  pl.BlockSpec((tm, kw), lambda i: (i, qw // kw + 1)),
                  tab, tab, tab, vec, vec],
        out_specs=(pl.BlockSpec((tm, qw), lambda i: (i, 0)), pl.BlockSpec((tm, kw), lambda i: (i, 0)),
                   pl.BlockSpec((tm, kw), lambda i: (i, 0))),
        compiler_params=_cparams("parallel"),
        name="qkv_prepare",
    )(proj, proj, proj, cos, sa, sb, q_g.reshape(1, HEAD_DIM), k_g.reshape(1, HEAD_DIM))


def _attn_kernel(sink_ref, q_ref, kp_ref, kc_ref, kn_ref, kx_ref, vp_ref, vc_ref, vn_ref, vx_ref, o_ref,
                 *, ctx_blocks, n_blocks):
    b = pl.program_id(0)
    blk = ATT_BLOCK
    rows = Q_PER_KV * blk
    n_ctx = ctx_blocks * blk
    latent = b >= ctx_blocks
    lat_f = jnp.where(latent, 1.0, 0.0)
    prev_f = jnp.where(jnp.logical_and(latent, b - 1 >= ctx_blocks), 1.0, 0.0)
    next_f = jnp.where(jnp.logical_and(latent, b + 1 <= n_blocks - 1), 1.0, 0.0)
    qi = lax.broadcasted_iota(jnp.int32, (rows, 3 * blk + n_ctx), 0) % blk
    col = lax.broadcasted_iota(jnp.int32, (rows, 3 * blk + n_ctx), 1)
    kj = col % blk
    allowed = jnp.where(col < blk, jnp.where(kj >= qi, prev_f, 0.0),
                        jnp.where(col < 2 * blk, lat_f,
                                  jnp.where(col < 3 * blk, jnp.where(kj <= qi, next_f, 0.0), 1.0))) > 0.5
    for h in range(KV_HEADS):
        hs = slice(h * HEAD_DIM, (h + 1) * HEAD_DIM)
        q4 = jnp.concatenate([q_ref[:, (Q_PER_KV * h + g) * HEAD_DIM:(Q_PER_KV * h + g + 1) * HEAD_DIM]
                              for g in range(Q_PER_KV)], axis=0)
        kk = jnp.concatenate([kp_ref[:, hs], kc_ref[:, hs], kn_ref[:, hs], kx_ref[:, hs]], axis=0)
        vv = jnp.concatenate([vp_ref[:, hs], vc_ref[:, hs], vn_ref[:, hs], vx_ref[:, hs]], axis=0)
        s = lax.dot_general(q4, kk, (((1,), (1,)), ((), ())), preferred_element_type=F32)
        s = jnp.where(allowed, s, NEG_INF)
        sk = jnp.concatenate([jnp.full((blk, 1), sink_ref[Q_PER_KV * h + g], F32) for g in range(Q_PER_KV)],
                             axis=0)
        m = jnp.maximum(jnp.max(s, axis=-1, keepdims=True), sk)
        p = jnp.exp(s - m)
        den = jnp.sum(p, axis=-1, keepdims=True) + jnp.exp(sk - m)
        o = jnp.dot(p.astype(BF16), vv, preferred_element_type=F32) / den
        for g in range(Q_PER_KV):
            o_ref[:, (Q_PER_KV * h + g) * HEAD_DIM:(Q_PER_KV * h + g + 1) * HEAD_DIM] = (
                o[g * blk:(g + 1) * blk, :].astype(BF16))


def _attention(q, k, v, sink, n_ctx):
    t = q.shape[0]
    blk = ATT_BLOCK
    nb = t // blk
    cb = n_ctx // blk
    kw = k.shape[1]
    prev = pl.BlockSpec((blk, kw), lambda b, s: (jnp.maximum(b - 1, 0), 0))
    cur = pl.BlockSpec((blk, kw), lambda b, s: (b, 0))
    nxt = pl.BlockSpec((blk, kw), lambda b, s: (jnp.minimum(b + 1, nb - 1), 0))
    ctx = pl.BlockSpec((n_ctx, kw), lambda b, s: (0, 0))
    return pl.pallas_call(
        functools.partial(_attn_kernel, ctx_blocks=cb, n_blocks=nb),
        out_shape=jax.ShapeDtypeStruct(q.shape, BF16),
        grid_spec=pltpu.PrefetchScalarGridSpec(
            num_scalar_prefetch=1, grid=(nb,),
            in_specs=[pl.BlockSpec((blk, q.shape[1]), lambda b, s: (b, 0)),
                      prev, cur, nxt, ctx, prev, cur, nxt, ctx],
            out_specs=pl.BlockSpec((blk, q.shape[1]), lambda b, s: (b, 0))),
        compiler_params=_cparams("parallel"),
        name="window_attention",
    )(sink, q, k, k, k, k, v, v, v, v)


def _pool_kernel(*refs, n_ctx, n_lat, n_tiles):
    ng = len(POOL_WINDOWS)
    cur_refs, prev_refs, next_refs = refs[:ng], refs[ng:2 * ng], refs[2 * ng:3 * ng]
    w_ref, scale_ref, o_ref, xe_ref = refs[3 * ng:]
    i = pl.program_id(0)
    tm = ROW_TILE
    hl = POOL_HALO
    gw = w_ref.shape[1]
    prev_valid = i >= 2
    next_valid = jnp.logical_and(i >= 1, i < n_tiles - 1)
    pos = lax.broadcasted_iota(jnp.int32, (tm, 1), 0) + jnp.where(i == 0, 0, (i - 1) * tm)
    seg_len = jnp.where(i == 0, n_ctx, n_lat)
    for g, win in enumerate(POOL_WINDOWS):
        cur = cur_refs[g][...]
        xe_ref[0:hl, :] = jnp.where(prev_valid, prev_refs[g][...], 0.0)
        xe_ref[hl:hl + tm, :] = cur
        xe_ref[hl + tm:hl + tm + hl, :] = jnp.where(next_valid, next_refs[g][...], 0.0)
        acc = xe_ref[hl - win // 2:hl - win // 2 + tm, :]
        for o in range(1 - win // 2, win // 2):
            acc = acc + xe_ref[hl + o:hl + o + tm, :]
        cnt = jnp.minimum(pos - win // 2 + win, seg_len) - jnp.maximum(pos - win // 2, 0)
        d = acc / cnt.astype(F32) - cur
        y = jnp.dot(d.astype(BF16), w_ref[g].astype(BF16), preferred_element_type=F32)
        o_ref[:, g * gw:(g + 1) * gw] = (y * scale_ref[:, g * gw:(g + 1) * gw]).astype(BF16)


def _pool_mixer(proj, col0, w_pool, pool_scale, n_ctx):
    t = proj.shape[0]
    tm, hl = ROW_TILE, POOL_HALO
    ng, gw = w_pool.shape[0], w_pool.shape[1]
    nt = t // tm
    cb = col0 // gw
    per = tm // hl
    cur = [pl.BlockSpec((tm, gw), functools.partial(lambda i, g: (i, cb + g), g=g)) for g in range(ng)]
    prev = [pl.BlockSpec((hl, gw), functools.partial(lambda i, g: (jnp.maximum(i * per - 1, 0), cb + g), g=g))
            for g in range(ng)]
    nxt = [pl.BlockSpec((hl, gw), functools.partial(lambda i, g: (jnp.minimum((i + 1) * per, nt * per - 1), cb + g), g=g))
           for g in range(ng)]
    return pl.pallas_call(
        functools.partial(_pool_kernel, n_ctx=n_ctx, n_lat=t - n_ctx, n_tiles=nt),
        out_shape=jax.ShapeDtypeStruct((t, ng * gw), BF16),
        grid=(nt,),
        in_specs=cur + prev + nxt + [pl.BlockSpec((ng, gw, gw), lambda i: (0, 0, 0)),
                                     pl.BlockSpec((1, ng * gw), lambda i: (0, 0))],
        out_specs=pl.BlockSpec((tm, ng * gw), lambda i: (i, 0)),
        scratch_shapes=[pltpu.VMEM((tm + 2 * hl, gw), F32)],
        compiler_params=_cparams("parallel"),
        name="pool_mixer",
    )(*([proj] * (3 * ng)), w_pool, pool_scale.reshape(1, ng * gw))


def _s5_powers(lr, li, dt, kcol):
    mag = jnp.exp(kcol * (lr * dt))
    ang = kcol * (li * dt)
    return mag * jnp.cos(ang), mag * jnp.sin(ang)


def _s5_expand(z_re, z_im, e_re, e_im):
    n, p = e_re.shape
    hh = z_re.shape[0]
    er = jnp.broadcast_to(e_re[:, None, :], (n, hh, p)).reshape(n * hh, p)
    ei = jnp.broadcast_to(e_im[:, None, :], (n, hh, p)).reshape(n * hh, p)
    zr = jnp.broadcast_to(z_re[None, :, :], (n, hh, p)).reshape(n * hh, p)
    zi = jnp.broadcast_to(z_im[None, :, :], (n, hh, p)).reshape(n * hh, p)
    return zr * er - zi * ei, zr * ei + zi * er


def _s5_discretize_t(lr, li, dt, bt_re, bt_im):
    mag = jnp.exp(lr * dt)
    a_re, a_im = mag * jnp.cos(li * dt), mag * jnp.sin(li * dt)
    den = lr * lr + li * li
    n_re = a_re - 1.0
    k_re = (n_re * lr + a_im * li) / den
    k_im = (a_im * lr - n_re * li) / den
    return k_re * bt_re - k_im * bt_im, k_re * bt_im + k_im * bt_re


def _s5_group_params(lam_re_ref, lam_im_ref, ldt_ref, bt_re_ref, bt_im_ref, d):
    lr, li = lam_re_ref[d], lam_im_ref[d]
    dt = jnp.exp(ldt_ref[d])
    bb_re, bb_im = _s5_discretize_t(lr, li, dt, bt_re_ref[d], bt_im_ref[d])
    return lr, li, dt, bb_re, bb_im


def _s5_state_kernel(u_ref, lam_re_ref, lam_im_ref, ldt_ref, bt_re_ref, bt_im_ref, sf_ref, sb_ref):
    lc = S5_CHUNK
    u = u_ref[...].astype(BF16)
    t_col = lax.broadcasted_iota(jnp.int32, (lc, 1), 0).astype(F32)
    for d, out_ref, kcol in ((0, sf_ref, (lc - 1) - t_col), (1, sb_ref, t_col)):
        lr, li, dt, bb_re, bb_im = _s5_group_params(lam_re_ref, lam_im_ref, ldt_ref, bt_re_ref, bt_im_ref, d)
        e_re, e_im = _s5_powers(lr, li, dt, kcol)
        w_re, w_im = _s5_expand(bb_re, bb_im, e_re, e_im)
        w = jnp.concatenate([w_re, w_im], axis=1).astype(BF16)
        out_ref[...] = jnp.dot(u, w, preferred_element_type=F32)


def _s5_carry_kernel(sf_ref, sb_ref, lam_re_ref, lam_im_ref, ldt_ref, hf_ref, hb_ref, sw_ref, *, ctx_chunks):
    nc = sf_ref.shape[0]
    lanes = sf_ref.shape[1]
    first = (lax.broadcasted_iota(jnp.int32, (1, lanes), 1) % (2 * S5_P)) < S5_P

    def scan(d, s_ref, h_ref, order):
        lr, li, dt = lam_re_ref[d], lam_im_ref[d], jnp.exp(ldt_ref[d])
        mag = jnp.exp(S5_CHUNK * (lr * dt))
        ang = S5_CHUNK * (li * dt)
        pa = mag * jnp.cos(ang)
        a_im = mag * jnp.sin(ang)
        qa = jnp.where(first, -a_im, a_im)
        sw_ref[...] = pltpu.roll(s_ref[...], S5_P, 1)
        zero = jnp.zeros((1, lanes), F32)

        def run(lo, n, step, carry):
            def body(k, hc):
                h, hs = hc
                c = lo + k * step
                h_ref[pl.ds(c, 1), :] = h
                s = s_ref[pl.ds(c, 1), :]
                ss = sw_ref[pl.ds(c, 1), :]
                return pa * h + qa * hs + s, pa * hs - qa * h + ss
            return lax.fori_loop(0, n, body, carry)

        carry = (zero, zero)
        for lo, n, step in order:
            carry = run(lo, n, step, carry)

    scan(0, sf_ref, hf_ref, ((0, nc, 1),))
    scan(1, sb_ref, hb_ref, ((ctx_chunks - 1, ctx_chunks, -1), (nc - 1, nc - ctx_chunks, -1)))


def _s5_output_kernel(u_ref, hf_ref, hb_ref, lam_re_ref, lam_im_ref, ldt_ref, bt_re_ref, bt_im_ref,
                      c_re_ref, c_im_ref, y_ref):
    lc = S5_CHUNK
    half = lc // 2
    n = lc * S5_H
    u = u_ref[...].astype(BF16)
    t_col = lax.broadcasted_iota(jnp.int32, (lc, 1), 0).astype(F32)
    s_idx = lax.broadcasted_iota(jnp.int32, (n, n), 0) // S5_H
    t_idx = lax.broadcasted_iota(jnp.int32, (n, n), 1) // S5_H
    mt = jnp.zeros((n, n), F32)
    y = jnp.zeros((u.shape[0], n), F32)
    for d, h_ref in ((0, hf_ref), (1, hb_ref)):
        lr, li, dt, bb_re, bb_im = _s5_group_params(lam_re_ref, lam_im_ref, ldt_ref, bt_re_ref, bt_im_ref, d)
        c_re, c_im = c_re_ref[d], c_im_ref[d]
        if d == 0:
            kl, kr, kc = t_col - half, half - t_col, t_col + 1.0
            keep = s_idx <= t_idx
        else:
            kl, kr, kc = half - t_col, t_col - half, lc - t_col
            keep = s_idx >= t_idx
        l_re, l_im = _s5_expand(c_re, c_im, *_s5_powers(lr, li, dt, kl))
        r_re, r_im = _s5_expand(bb_re, bb_im, *_s5_powers(lr, li, dt, kr))
        lst = jnp.concatenate([l_re, -l_im], axis=1)
        rst = jnp.concatenate([r_re, r_im], axis=1)
        r3, l3 = _split3(rst), _split3(lst)
        m = jnp.zeros((n, n), F32)
        for ra, la in ((0, 0), (0, 1), (1, 0), (0, 2), (1, 1), (2, 0)):
            m = m + lax.dot_general(r3[ra], l3[la], (((1,), (1,)), ((), ())), preferred_element_type=F32)
        mt = mt + jnp.where(keep, m, 0.0)
        ca_re, ca_im = _s5_expand(c_re, c_im, *_s5_powers(lr, li, dt, kc))
        t2 = jnp.concatenate([ca_re, -ca_im], axis=1).astype(BF16)
        y = y + lax.dot_general(h_ref[...].astype(BF16), t2, (((1,), (1,)), ((), ())),
                                preferred_element_type=F32)
    y_ref[...] = y + jnp.dot(u, mt.astype(BF16), preferred_element_type=F32)


def _s5_scan(s_all, n_ctx, lam_re, lam_im, log_dt, b_re, b_im, c_re, c_im):
    t, cw = s_all.shape
    g, p, h, lc = cw // S5_H, S5_P, S5_H, S5_CHUNK
    nc = t // lc
    ug = s_all.reshape(nc, lc, g, h).transpose(2, 0, 1, 3).reshape(g, nc, lc * h)
    lam_re4 = lam_re.transpose(1, 0, 2)[:, :, None, :]
    lam_im4 = lam_im.transpose(1, 0, 2)[:, :, None, :]
    ldt4 = jnp.broadcast_to(log_dt.T[:, :, None, None], (g, 2, 1, p))
    bt_re = b_re.transpose(1, 0, 3, 2)
    bt_im = b_im.transpose(1, 0, 3, 2)
    c_re4 = c_re.transpose(1, 0, 2, 3)
    c_im4 = c_im.transpose(1, 0, 2, 3)
    vec4 = pl.BlockSpec((None, 2, 1, p), lambda i: (i, 0, 0, 0))
    mat4 = pl.BlockSpec((None, 2, h, p), lambda i: (i, 0, 0, 0))
    u_spec = pl.BlockSpec((None, nc, lc * h), lambda i: (i, 0, 0))
    st_spec = pl.BlockSpec((nc, 2 * p), lambda i: (0, i))
    st_shape = jax.ShapeDtypeStruct((nc, g * 2 * p), F32)
    s_f, s_b = pl.pallas_call(
        _s5_state_kernel, out_shape=(st_shape, st_shape), grid=(g,),
        in_specs=[u_spec, vec4, vec4, vec4, mat4, mat4],
        out_specs=(st_spec, st_spec),
        compiler_params=_cparams("parallel"), name="s5_chunk_states",
    )(ug, lam_re4, lam_im4, ldt4, bt_re, bt_im)

    def lanes(x):
        return jnp.concatenate([x, x], axis=-1).reshape(2, 1, g * 2 * p)
    cols = 8 * 2 * p
    row = pl.BlockSpec((2, 1, cols), lambda j: (0, 0, j))
    blk = pl.BlockSpec((nc, cols), lambda j: (0, j))
    h_f, h_b = pl.pallas_call(
        functools.partial(_s5_carry_kernel, ctx_chunks=n_ctx // lc),
        out_shape=(st_shape, st_shape), grid=(g * 2 * p // cols,),
        in_specs=[blk, blk, row, row, row], out_specs=(blk, blk),
        scratch_shapes=[pltpu.VMEM((nc, cols), F32)],
        compiler_params=_cparams("parallel"), name="s5_chunk_carry",
    )(s_f, s_b, lanes(lam_re), lanes(lam_im), lanes(jnp.broadcast_to(log_dt[:, :, None], (2, g, p))))

    yg = pl.pallas_call(
        _s5_output_kernel, out_shape=jax.ShapeDtypeStruct((g, nc, lc * h), F32), grid=(g,),
        in_specs=[u_spec, st_spec, st_spec, vec4, vec4, vec4, mat4, mat4, mat4, mat4],
        out_specs=u_spec,
        compiler_params=_cparams("parallel"), name="s5_outputs",
    )(ug, h_f, h_b, lam_re4, lam_im4, ldt4, bt_re, bt_im, c_re4, c_im4)
    return yg.reshape(g, nc, lc, h).transpose(1, 2, 0, 3).reshape(t, cw)


def _glu_kernel(y_ref, u_ref, dk_ref, w_ref, b_ref, o_ref, wbf_ref):
    @pl.when(pl.program_id(0) == 0)
    def _():
        wbf_ref[...] = w_ref[...].astype(BF16)

    yl = dk_ref[...] * u_ref[...] + y_ref[...]
    gl = _gelu(yl)
    z = jnp.dot(gl.astype(BF16), wbf_ref[...], preferred_element_type=F32) + b_ref[...]
    o_ref[...] = (gl * _sigmoid(z)).astype(BF16)


def _s5_glu(y_scan, proj, d_skip, glu_w, glu_b, n_ctx):
    t, cw = y_scan.shape
    tm = ROW_TILE
    off = n_ctx // tm
    n_lat = t - n_ctx
    vec = pl.BlockSpec((1, cw), lambda i: (0, 0))
    return pl.pallas_call(
        _glu_kernel, out_shape=jax.ShapeDtypeStruct((n_lat, cw), BF16), grid=(n_lat // tm,),
        in_specs=[pl.BlockSpec((tm, cw), lambda i: (i + off, 0)),
                  pl.BlockSpec((tm, cw), lambda i: (i + off, 0)),
                  vec, pl.BlockSpec((cw, cw), lambda i: (0, 0)), vec],
        out_specs=pl.BlockSpec((tm, cw), lambda i: (i, 0)),
        scratch_shapes=[pltpu.VMEM((cw, cw), BF16)],
        compiler_params=_cparams("arbitrary"), name="s5_glu",
    )(y_scan, proj, d_skip.reshape(1, cw), glu_w, glu_b.reshape(1, cw))


def _gmlp_kernel(*refs, n_col):
    u_refs, v_refs = refs[:n_col], refs[n_col:2 * n_col]
    lng_ref, lnb_ref, ws_ref, bst_ref, o_ref = refs[2 * n_col:]
    u = _gelu(jnp.concatenate([r[...] for r in u_refs], axis=1))
    v = _gelu(jnp.concatenate([r[...] for r in v_refs], axis=1))
    mu = jnp.mean(v, axis=-1, keepdims=True)
    vc = v - mu
    var = jnp.mean(vc * vc, axis=-1, keepdims=True)
    vn = (vc * lax.rsqrt(var + EPS) * lng_ref[...] + lnb_ref[...]).astype(BF16)
    gw = u.shape[1] // SG_HEADS
    for c in range(u.shape[0] // SG_CHUNK):
        rs = slice(c * SG_CHUNK, (c + 1) * SG_CHUNK)
        for g in range(SG_HEADS):
            cs = slice(g * gw, (g + 1) * gw)
            mixed = jnp.dot(ws_ref[g].astype(BF16), vn[rs, cs], preferred_element_type=F32)
            mixed = mixed + bst_ref[:, g:g + 1]
            o_ref[rs, cs] = (u[rs, cs] * mixed).astype(BF16)


def _gmlp_mixer(proj, col_u, col_v, width, ln_g, ln_b, w_s, b_s, n_ctx):
    t = proj.shape[0]
    tm, cwid = ROW_TILE, 1024
    off = n_ctx // tm
    n_lat = t - n_ctx
    n_col = width // cwid
    specs = [pl.BlockSpec((tm, cwid), functools.partial(lambda i, cb: (i + off, cb), cb=(c0 // cwid) + k))
             for c0 in (col_u, col_v) for k in range(n_col)]
    vec = pl.BlockSpec((1, width), lambda i: (0, 0))
    return pl.pallas_call(
        functools.partial(_gmlp_kernel, n_col=n_col),
        out_shape=jax.ShapeDtypeStruct((n_lat, width), BF16), grid=(n_lat // tm,),
        in_specs=specs + [vec, vec, pl.BlockSpec(w_s.shape, lambda i: (0, 0, 0)),
                          pl.BlockSpec((SG_CHUNK, SG_HEADS), lambda i: (0, 0))],
        out_specs=pl.BlockSpec((tm, width), lambda i: (i, 0)),
        compiler_params=_cparams("parallel"), name="gmlp_mixer",
    )(*([proj] * (2 * n_col)), ln_g.reshape(1, width), ln_b.reshape(1, width), w_s, b_s.T)


def _router_kernel(x_ref, g_ref, sh_ref, sc_ref, wr_ref, br_ref, hp_ref, rt_ref, w3_ref):
    @pl.when(pl.program_id(0) == 0)
    def _():
        hi, mid, lo = _split3(wr_ref[...])
        w3_ref[0], w3_ref[1], w3_ref[2] = hi, mid, lo

    h = _modulated_norm(x_ref[...], g_ref, sh_ref, sc_ref)
    half = h.shape[1] // 2
    hp_ref[...] = pltpu.pack_elementwise([h[:, :half], h[:, half:]], packed_dtype=jnp.bfloat16)

    h3 = _split3(h)
    logits = br_ref[...]
    for a, b in ((0, 0), (0, 1), (1, 0), (0, 2), (1, 1), (2, 0)):
        logits = logits + jnp.dot(h3[a], w3_ref[b], preferred_element_type=F32)

    lane = lax.broadcasted_iota(jnp.int32, logits.shape, 1)
    big = jnp.int32(ROUTER_LANES)
    gl = jnp.where(lane < MOE_GROUPS, logits, NEG_INF)
    gm = jnp.max(gl, axis=-1, keepdims=True)
    g_w = 1.0 / jnp.sum(jnp.exp(gl - gm), axis=-1, keepdims=True)
    g_idx = jnp.min(jnp.where(gl == gm, lane, big), axis=-1, keepdims=True)
    lo_lane = MOE_GROUPS + g_idx * MOE_PER_GROUP
    fl = jnp.where(jnp.logical_and(lane >= lo_lane, lane < lo_lane + MOE_PER_GROUP), logits, NEG_INF)
    m1 = jnp.max(fl, axis=-1, keepdims=True)
    i1 = jnp.min(jnp.where(fl == m1, lane, big), axis=-1, keepdims=True)
    fl2 = jnp.where(lane == i1, NEG_INF, fl)
    m2 = jnp.max(fl2, axis=-1, keepdims=True)
    i2 = jnp.min(jnp.where(fl2 == m2, lane, big), axis=-1, keepdims=True)
    e2 = jnp.exp(m2 - m1)
    w1 = g_w / (1.0 + e2)
    w2 = g_w * e2 / (1.0 + e2)
    rt_ref[...] = jnp.where(lane == 0, (i1 - MOE_GROUPS).astype(F32),
                            jnp.where(lane == 1, (i2 - MOE_GROUPS).astype(F32),
                                      jnp.where(lane == 2, w1, jnp.where(lane == 3, w2, 0.0))))


def _moe_router(xt, tile_off, n_rows, g, shifts, scales, first_latent_tile, w_grp, b_grp, w_rt, b_rt):
    d = xt.shape[1]
    tm = ROW_TILE
    ne = MOE_GROUPS * MOE_PER_GROUP
    wr = jnp.concatenate([w_grp, w_rt.transpose(1, 0, 2).reshape(d, ne),
                          jnp.zeros((d, ROUTER_LANES - MOE_GROUPS - ne), F32)], axis=1)
    br = jnp.concatenate([b_grp, b_rt.reshape(ne), jnp.zeros((ROUTER_LANES - MOE_GROUPS - ne,), F32)])
    sel = lambda i: (jnp.where(i + tile_off >= first_latent_tile, 1, 0), 0, 0)
    return pl.pallas_call(
        _router_kernel,
        out_shape=(jax.ShapeDtypeStruct((n_rows, d // 2), jnp.int32),
                   jax.ShapeDtypeStruct((n_rows, ROUTER_LANES), F32)),
        grid=(n_rows // tm,),
        in_specs=[pl.BlockSpec((tm, d), lambda i: (i + tile_off, 0)),
                  pl.BlockSpec((1, d), lambda i: (0, 0)),
                  pl.BlockSpec((None, 1, d), sel), pl.BlockSpec((None, 1, d), sel),
                  pl.BlockSpec((d, ROUTER_LANES), lambda i: (0, 0)),
                  pl.BlockSpec((1, ROUTER_LANES), lambda i: (0, 0))],
        out_specs=(pl.BlockSpec((tm, d // 2), lambda i: (i, 0)),
                   pl.BlockSpec((tm, ROUTER_LANES), lambda i: (i, 0))),
        scratch_shapes=[pltpu.VMEM((3, d, ROUTER_LANES), BF16)],
        compiler_params=_cparams("arbitrary"), name="moe_router",
    )(xt, g.reshape(1, d), shifts, scales, wr, br.reshape(1, ROUTER_LANES))


def _row_gather_kernel(nu_ref, idx_ref, src_hbm, out_ref, sem):
    t = pl.program_id(0)
    tm = out_ref.shape[0]

    @pl.when(t < nu_ref[0])
    def _():
        def issue(r, carry):
            tok = idx_ref[t * tm + r]
            pltpu.make_async_copy(src_hbm.at[pl.ds(tok, 1)], out_ref.at[pl.ds(r, 1)], sem).start()
            return carry

        lax.fori_loop(0, tm, issue, 0, unroll=DMA_ISSUE_UNROLL)
        pltpu.make_async_copy(src_hbm.at[pl.ds(0, tm)], out_ref, sem).wait()


def _moe_dispatch(hp, src_tok, n_used, n_tiles):
    tm = MOE_TILE
    w = hp.shape[1]
    last = lambda t, nu, idx: (jnp.minimum(t, nu[0] - 1), 0)
    return pl.pallas_call(
        _row_gather_kernel,
        out_shape=jax.ShapeDtypeStruct((n_tiles * tm, w), hp.dtype),
        grid_spec=pltpu.PrefetchScalarGridSpec(
            num_scalar_prefetch=2, grid=(n_tiles,),
            in_specs=[pl.BlockSpec(memory_space=pl.ANY)],
            out_specs=pl.BlockSpec((tm, w), last),
            scratch_shapes=[pltpu.SemaphoreType.DMA(())]),
        compiler_params=_cparams("arbitrary", unchecked_dma=True), name="moe_dispatch",
    )(n_used, src_tok, hp)


def _unpack_pairs(xp):
    lo = pltpu.unpack_elementwise(xp, index=0, packed_dtype=jnp.bfloat16, unpacked_dtype=F32)
    hi = pltpu.unpack_elementwise(xp, index=1, packed_dtype=jnp.bfloat16, unpacked_dtype=F32)
    return lo.astype(BF16), hi.astype(BF16)


def _expert_up_kernel(te_ref, nu_ref, xs_ref, wg_ref, wu_ref, h_ref, wgb_ref, wub_ref):
    t = pl.program_id(0)
    tc = jnp.minimum(t, nu_ref[0] - 1)
    new_expert = jnp.logical_or(t == 0, te_ref[tc] != te_ref[jnp.maximum(tc - 1, 0)])

    @pl.when(t < nu_ref[0])
    def _():
        @pl.when(new_expert)
        def _():
            wgb_ref[...] = wg_ref[...].astype(BF16)
            wub_ref[...] = wu_ref[...].astype(BF16)

        lo, hi = _unpack_pairs(xs_ref[...])
        half = lo.shape[1]
        gate = (jnp.dot(lo, wgb_ref[:half, :], preferred_element_type=F32)
                + jnp.dot(hi, wgb_ref[half:, :], preferred_element_type=F32))
        up = (jnp.dot(lo, wub_ref[:half, :], preferred_element_type=F32)
              + jnp.dot(hi, wub_ref[half:, :], preferred_element_type=F32))
        h_ref[...] = (gate * _sigmoid(gate) * up).astype(BF16)


def _expert_down_kernel(te_ref, nu_ref, h_ref, rw_ref, wd_ref, y_ref, wdb_ref):
    t = pl.program_id(0)
    tc = jnp.minimum(t, nu_ref[0] - 1)
    new_expert = jnp.logical_or(t == 0, te_ref[tc] != te_ref[jnp.maximum(tc - 1, 0)])

    @pl.when(t < nu_ref[0])
    def _():
        @pl.when(new_expert)
        def _():
            wdb_ref[...] = wd_ref[...].astype(BF16)

        y_ref[...] = jnp.dot(h_ref[...], wdb_ref[...], preferred_element_type=F32) * rw_ref[...]


def _moe_experts(xs, row_w, tile_expert, n_used, layer, w_gate, w_up, w_down):
    tm = MOE_TILE
    npad, half = xs.shape
    d = 2 * half
    ff = w_gate.shape[3]
    nt = npad // tm
    tile = lambda t, nu: jnp.minimum(t, nu[0] - 1)
    hmid = pl.pallas_call(
        _expert_up_kernel,
        out_shape=jax.ShapeDtypeStruct((npad, ff), BF16),
        grid_spec=pltpu.PrefetchScalarGridSpec(
            num_scalar_prefetch=2, grid=(nt,),
            in_specs=[pl.BlockSpec((tm, half), lambda t, te, nu: (tile(t, nu), 0)),
                      pl.BlockSpec((None, None, d, ff), lambda t, te, nu: (layer, te[tile(t, nu)], 0, 0)),
                      pl.BlockSpec((None, None, d, ff), lambda t, te, nu: (layer, te[tile(t, nu)], 0, 0))],
            out_specs=pl.BlockSpec((tm, ff), lambda t, te, nu: (tile(t, nu), 0)),
            scratch_shapes=[pltpu.VMEM((d, ff), BF16), pltpu.VMEM((d, ff), BF16)]),
        compiler_params=_cparams("arbitrary"), name="moe_expert_up",
    )(tile_expert, n_used, xs, w_gate, w_up)
    return pl.pallas_call(
        _expert_down_kernel,
        out_shape=jax.ShapeDtypeStruct((npad, d), F32),
        grid_spec=pltpu.PrefetchScalarGridSpec(
            num_scalar_prefetch=2, grid=(nt,),
            in_specs=[pl.BlockSpec((tm, ff), lambda t, te, nu: (tile(t, nu), 0)),
                      pl.BlockSpec((tm, 1), lambda t, te, nu: (tile(t, nu), 0)),
                      pl.BlockSpec((None, None, ff, d), lambda t, te, nu: (layer, te[tile(t, nu)], 0, 0))],
            out_specs=pl.BlockSpec((tm, d), lambda t, te, nu: (tile(t, nu), 0)),
            scratch_shapes=[pltpu.VMEM((ff, d), BF16)]),
        compiler_params=_cparams("arbitrary"), name="moe_expert_down",
    )(tile_expert, n_used, hmid, row_w, w_down)


def _combine_kernel(pos_ref, y_hbm, res_ref, gate_ref, o_ref, buf_ref, sem):
    i = pl.program_id(0)
    tm = o_ref.shape[0]

    def issue(r, carry):
        for k in range(2):
            p = pos_ref[2 * (i * tm + r) + k]
            pltpu.make_async_copy(y_hbm.at[pl.ds(p, 1)], buf_ref.at[k, pl.ds(r, 1)], sem).start()
        return carry

    lax.fori_loop(0, tm, issue, 0, unroll=DMA_ISSUE_UNROLL)
    pltpu.make_async_copy(y_hbm.at[pl.ds(0, 2 * tm)], buf_ref.reshape(2 * tm, buf_ref.shape[2]), sem).wait()
    o_ref[...] = res_ref[...] + gate_ref[...] * (buf_ref[0] + buf_ref[1])


def _moe_combine(y_sorted, pos, res, res_tile_off, n_rows, gates, first_latent_tile):
    d = y_sorted.shape[1]
    tm = ROW_TILE
    sel = lambda i, p: (jnp.where(i + res_tile_off >= first_latent_tile, 1, 0), 0, 0)
    return pl.pallas_call(
        _combine_kernel,
        out_shape=jax.ShapeDtypeStruct((n_rows, d), F32),
        grid_spec=pltpu.PrefetchScalarGridSpec(
            num_scalar_prefetch=1, grid=(n_rows // tm,),
            in_specs=[pl.BlockSpec(memory_space=pl.ANY),
                      pl.BlockSpec((tm, d), lambda i, p: (i + res_tile_off, 0)),
                      pl.BlockSpec((None, 1, d), sel)],
            out_specs=pl.BlockSpec((tm, d), lambda i, p: (i, 0)),
            scratch_shapes=[pltpu.VMEM((2, tm, d), F32), pltpu.SemaphoreType.DMA(())]),
        compiler_params=_cparams("arbitrary", unchecked_dma=True), name="moe_combine",
    )(pos.reshape(-1), y_sorted, res, gates)


def _routing_plan(route, n_tiles):
    tm = MOE_TILE
    n = route.shape[0]
    ids = route[:, 0:2].astype(jnp.int32).reshape(-1)
    wts = route[:, 2:4].reshape(-1)
    onehot = (ids[:, None] == jnp.arange(MOE_EXPERTS)[None, :]).astype(jnp.int32)
    rank = jnp.sum(jnp.cumsum(onehot, axis=0) * onehot, axis=1) - 1
    counts = jnp.sum(onehot, axis=0)
    tiles = (counts + tm - 1) // tm
    tile_end = jnp.cumsum(tiles)
    tile_start = tile_end - tiles
    dense_start = jnp.cumsum(counts) - counts
    pos = tile_start[ids] * tm + rank
    order = jnp.zeros((2 * n,), jnp.int32).at[dense_start[ids] + rank].set(jnp.arange(2 * n, dtype=jnp.int32))
    n_used = tile_end[-1].astype(jnp.int32).reshape(1)
    tile_ids = jnp.arange(n_tiles, dtype=jnp.int32)
    tile_expert = jnp.minimum(jnp.sum((tile_end[None, :] <= tile_ids[:, None]).astype(jnp.int32), axis=1),
                              MOE_EXPERTS - 1)
    row = jnp.arange(n_tiles * tm, dtype=jnp.int32)
    row_e = tile_expert[row // tm]
    local = row - tile_start[row_e] * tm
    valid = jnp.logical_and(local < counts[row_e], row // tm < n_used[0])
    assign = order[jnp.clip(dense_start[row_e] + local, 0, 2 * n - 1)]
    src_tok = jnp.where(valid, assign // 2, 0)
    row_w = jnp.where(valid, wts[assign], 0.0)
    return pos.astype(jnp.int32), src_tok, row_w.reshape(-1, 1), tile_expert, n_used


def _hier_moe(xt, tile_off, n_rows, norm_g, shifts, scales, gates, first_latent_tile, layer, moe):
    w_grp, b_grp, w_rt, b_rt, w_gate, w_up, w_down = moe
    hp, route = _moe_router(xt, tile_off, n_rows, norm_g, shifts, scales, first_latent_tile,
                            w_grp, b_grp, w_rt, b_rt)
    n_tiles = (2 * n_rows) // MOE_TILE + MOE_EXPERTS
    pos, src_tok, row_w, tile_expert, n_used = _routing_plan(route, n_tiles)
    xs = _moe_dispatch(hp, src_tok, n_used, n_tiles)
    y_sorted = _moe_experts(xs, row_w, tile_expert, n_used, layer, w_gate, w_up, w_down)
    return _moe_combine(y_sorted, pos, xt, tile_off, n_rows, gates, first_latent_tile)


def kernel(x, c, ctx, c_ctx, mod_w, mod_b, norm1_g, norm2_g, ab_w_in, ab_q_norm, ab_k_norm, ab_sink, ab_pool_w, ab_pool_scale, ab_w_out, cd_w_in, s5_lam_re, s5_lam_im, s5_log_dt, s5_b_re, s5_b_im, s5_c_re, s5_c_im, s5_d, s5_glu_w, s5_glu_b, sg_ln_g, sg_ln_b, sg_w, sg_b, cd_w_out, moe_w_group, moe_b_group, moe_w_router, moe_b_router, moe_w_gate, moe_w_up, moe_w_down):
    batch, n_lat, d = x.shape
    n_ctx = ctx.shape[1]
    assert batch == 1 and n_ctx == ROW_TILE and n_lat % ROW_TILE == 0 and mod_w.shape[0] == 2
    ctx_tiles = n_ctx // ROW_TILE

    cvecs = jnp.concatenate([c_ctx[None, :], c, jnp.zeros((6, d), F32)], axis=0)
    mod = _modulation(cvecs, mod_w, mod_b)[:, 0:2, :].reshape(2, 2, 6, 1, d)
    xt = jnp.concatenate([ctx[0], x[0]], axis=0)

    def moe_params(layer):
        return (moe_w_group[layer], moe_b_group[layer], moe_w_router[layer], moe_b_router[layer],
                moe_w_gate, moe_w_up, moe_w_down)

    m0 = mod[0]
    sh1, sc1, g1, sh2, sc2, g2 = (m0[:, k] for k in range(6))
    h = _norm_modulate(xt, norm1_g[0], sh1, sc1)
    proj = _in_projection(h, ab_w_in[0])
    q, k, v = _qkv_prepare(proj, ab_q_norm[0], ab_k_norm[0], n_ctx)
    att = _attention(q, k, v, ab_sink[0], n_ctx)
    pooled = _pool_mixer(proj, Q_HEADS * HEAD_DIM + 2 * KV_HEADS * HEAD_DIM, ab_pool_w[0], ab_pool_scale[0], n_ctx)
    xt = _out_projection(att, pooled, ab_w_out[0], xt, g1, 0, lambda i: jnp.minimum(i, 1))
    xt = _hier_moe(xt, 0, n_ctx + n_lat, norm2_g[0], sh2, sc2, g2, ctx_tiles, 0, moe_params(0))

    m1 = mod[1]
    sh1, sc1, g1, sh2, sc2, g2 = (m1[:, k] for k in range(6))
    h = _norm_modulate(xt, norm1_g[1], sh1, sc1)
    proj = _in_projection(h, cd_w_in[0])
    cw = s5_d.shape[1]
    dw = sg_ln_g.shape[1]
    y_scan = _s5_scan(proj[:, :cw], n_ctx, s5_lam_re[0], s5_lam_im[0], s5_log_dt[0], s5_b_re[0], s5_b_im[0],
                      s5_c_re[0], s5_c_im[0])
    y_c = _s5_glu(y_scan, proj, s5_d[0], s5_glu_w[0], s5_glu_b[0], n_ctx)
    y_d = _gmlp_mixer(proj, cw, cw + dw, dw, sg_ln_g[0], sg_ln_b[0], sg_w[0], sg_b[0], n_ctx)
    xl = _out_projection(y_c, y_d, cd_w_out[0], xt, g1, ctx_tiles, lambda i: 1)
    out = _hier_moe(xl, 0, n_lat, norm2_g[1], sh2, sc2, g2, 0, 1, moe_params(1))
    return out[None]
```

```python
import functools
import math

import jax
import jax.numpy as jnp
from jax import lax
from jax.experimental import pallas as pl
from jax.experimental.pallas import tpu as pltpu

F32 = jnp.float32
BF16 = jnp.bfloat16

EPS = 1e-6
NEG_INF = -1e30
GRID_W = 64
HEAD_DIM = 128
Q_HEADS = 16
KV_HEADS = 4
Q_PER_KV = Q_HEADS // KV_HEADS
ATT_BLOCK = 128
ROPE_BASE = 10000.0
ROPE_PAIRS = HEAD_DIM // 4
POOL_WINDOWS = (2, 4, 8, 16)
POOL_HALO = 8
S5_H = 16
S5_P = 64
S5_CHUNK = 32
SG_HEADS = 8
SG_CHUNK = 128
MOE_GROUPS = 4
MOE_PER_GROUP = 8
MOE_EXPERTS = MOE_GROUPS * MOE_PER_GROUP
LANES = 128
ROUTER_LANES = LANES
ROW_TILE = 256
MOE_TILE = 256
VMEM_LIMIT = 56 * 1024 * 1024


DMA_ISSUE_UNROLL = 8


def _cparams(*sem, unchecked_dma=False):
    return pltpu.CompilerParams(dimension_semantics=sem, vmem_limit_bytes=VMEM_LIMIT,
                                disable_bounds_checks=unchecked_dma)


def _sigmoid(x):
    return 1.0 / (1.0 + jnp.exp(-x))


def _gelu(x):
    return 0.5 * x * (1.0 + jnp.tanh(math.sqrt(2.0 / math.pi) * (x + 0.044715 * (x * x * x))))


def _split3(x):
    hi = x.astype(BF16)
    r1 = x - hi.astype(F32)
    mid = r1.astype(BF16)
    lo = (r1 - mid.astype(F32)).astype(BF16)
    return hi, mid, lo


def _mod_kernel(c_ref, w_ref, b_ref, o_ref):
    cv = c_ref[...]
    s = (cv * _sigmoid(cv)).astype(BF16)
    o_ref[...] = jnp.dot(s, w_ref[...].astype(BF16), preferred_element_type=F32) + b_ref[...]


def _modulation(cvecs, mod_w, mod_b):
    depth, d, n = mod_w.shape
    tn = 512
    return pl.pallas_call(
        _mod_kernel,
        out_shape=jax.ShapeDtypeStruct((depth, 8, n), F32),
        grid=(depth, n // tn),
        in_specs=[pl.BlockSpec((8, d), lambda l, j: (0, 0)),
                  pl.BlockSpec((None, d, tn), lambda l, j: (l, 0, j)),
                  pl.BlockSpec((None, 1, tn), lambda l, j: (l, 0, j))],
        out_specs=pl.BlockSpec((None, 8, tn), lambda l, j: (l, 0, j)),
        compiler_params=_cparams("parallel", "parallel"),
        name="modulation",
    )(cvecs, mod_w, mod_b.reshape(depth, 1, n))


def _modulated_norm(x, g_ref, sh_ref, sc_ref):
    ms = jnp.mean(x * x, axis=-1, keepdims=True)
    return x * lax.rsqrt(ms + EPS) * g_ref[...] * (1.0 + sc_ref[...]) + sh_ref[...]


def _norm_kernel(x_ref, g_ref, sh_ref, sc_ref, o_ref):
    o_ref[...] = _modulated_norm(x_ref[...], g_ref, sh_ref, sc_ref).astype(BF16)


def _norm_modulate(xt, g, shifts, scales):
    t, d = xt.shape
    tm = ROW_TILE
    sel = lambda i: (jnp.minimum(i, 1), 0, 0)
    return pl.pallas_call(
        _norm_kernel,
        out_shape=jax.ShapeDtypeStruct((t, d), BF16),
        grid=(t // tm,),
        in_specs=[pl.BlockSpec((tm, d), lambda i: (i, 0)),
                  pl.BlockSpec((1, d), lambda i: (0, 0)),
                  pl.BlockSpec((None, 1, d), sel),
                  pl.BlockSpec((None, 1, d), sel)],
        out_specs=pl.BlockSpec((tm, d), lambda i: (i, 0)),
        compiler_params=_cparams("parallel"),
        name="norm_modulate",
    )(xt, g.reshape(1, d), shifts, scales)


def _inproj_kernel(h_ref, w_ref, o_ref, wbf_ref):
    @pl.when(pl.program_id(1) == 0)
    def _():
        wbf_ref[...] = w_ref[...].astype(BF16)

    o_ref[...] = jnp.dot(h_ref[...], wbf_ref[...], preferred_element_type=F32)


def _in_projection(h, w):
    t, d = h.shape
    n = w.shape[1]
    tm = 3 * ROW_TILE if t % (3 * ROW_TILE) == 0 else ROW_TILE
    tn = 512
    return pl.pallas_call(
        _inproj_kernel,
        out_shape=jax.ShapeDtypeStruct((t, n), F32),
        grid=(n // tn, t // tm),
        in_specs=[pl.BlockSpec((tm, d), lambda j, i: (i, 0)),
                  pl.BlockSpec((d, tn), lambda j, i: (0, j))],
        out_specs=pl.BlockSpec((tm, tn), lambda j, i: (i, j)),
        scratch_shapes=[pltpu.VMEM((d, tn), BF16)],
        compiler_params=_cparams("arbitrary", "arbitrary"),
        name="in_projection",
    )(h, w)


def _outproj_kernel(a1_ref, a2_ref, w_ref, res_ref, gate_ref, o_ref, wbf_ref, *, k1):
    @pl.when(pl.program_id(1) == 0)
    def _():
        wbf_ref[...] = w_ref[...].astype(BF16)

    acc = jnp.dot(a1_ref[...], wbf_ref[:k1, :], preferred_element_type=F32)
    acc = acc + jnp.dot(a2_ref[...], wbf_ref[k1:, :], preferred_element_type=F32)
    o_ref[...] = res_ref[...] + gate_ref[...] * acc


def _out_projection(a1, a2, w, res, gates, res_tile_off, gate_of_tile):
    r, k1 = a1.shape
    k2 = a2.shape[1]
    d = w.shape[1]
    tm, tn = ROW_TILE, 1024
    return pl.pallas_call(
        functools.partial(_outproj_kernel, k1=k1),
        out_shape=jax.ShapeDtypeStruct((r, d), F32),
        grid=(d // tn, r // tm),
        in_specs=[pl.BlockSpec((tm, k1), lambda j, i: (i, 0)),
                  pl.BlockSpec((tm, k2), lambda j, i: (i, 0)),
                  pl.BlockSpec((k1 + k2, tn), lambda j, i: (0, j)),
                  pl.BlockSpec((tm, tn), lambda j, i: (i + res_tile_off, j)),
                  pl.BlockSpec((None, 1, tn), lambda j, i: (gate_of_tile(i), 0, j))],
        out_specs=pl.BlockSpec((tm, tn), lambda j, i: (i, j)),
        scratch_shapes=[pltpu.VMEM((k1 + k2, tn), BF16)],
        compiler_params=_cparams("arbitrary", "arbitrary"),
        name="out_projection",
    )(a1, a2, w, res, gates)


def _qkv_kernel(q_ref, k_ref, v_ref, cos_ref, sa_ref, sb_ref, qg_ref, kg_ref, qo_ref, ko_ref, vo_ref):
    cos, sa, sb = cos_ref[...], sa_ref[...], sb_ref[...]

    def head(x, g, scale):
        ms = jnp.mean(x * x, axis=-1, keepdims=True)
        xn = x * lax.rsqrt(ms + EPS) * g
        r = xn * cos + pltpu.roll(xn, HEAD_DIM - ROPE_PAIRS, 1) * sa + pltpu.roll(xn, ROPE_PAIRS, 1) * sb
        return (r * scale).astype(BF16)

    for h in range(Q_HEADS):
        sl = slice(h * HEAD_DIM, (h + 1) * HEAD_DIM)
        qo_ref[:, sl] = head(q_ref[:, sl], qg_ref[...], HEAD_DIM ** -0.5)
    for h in range(KV_HEADS):
        sl = slice(h * HEAD_DIM, (h + 1) * HEAD_DIM)
        ko_ref[:, sl] = head(k_ref[:, sl], kg_ref[...], 1.0)
    vo_ref[...] = v_ref[...].astype(BF16)


def _rope_tables(n_ctx, n_lat):
    t = jnp.arange(n_lat)
    row = (t // GRID_W).astype(F32)
    col = (t % GRID_W).astype(F32)
    inv_freq = ROPE_BASE ** (-jnp.arange(ROPE_PAIRS, dtype=F32) / ROPE_PAIRS)
    ang_r = row[:, None] * inv_freq
    ang_c = col[:, None] * inv_freq
    ang = jnp.concatenate([ang_r, ang_r, ang_c, ang_c], axis=-1)
    ang = jnp.concatenate([jnp.zeros((n_ctx, HEAD_DIM), F32), ang], axis=0)
    cos, sin = jnp.cos(ang), jnp.sin(ang)
    first = (jnp.arange(HEAD_DIM) % (2 * ROPE_PAIRS)) < ROPE_PAIRS
    return cos, jnp.where(first, -sin, 0.0), jnp.where(first, 0.0, sin)


def _qkv_prepare(proj, q_g, k_g, n_ctx):
    t = proj.shape[0]
    tm = ROW_TILE
    qw, kw = Q_HEADS * HEAD_DIM, KV_HEADS * HEAD_DIM
    cos, sa, sb = _rope_tables(n_ctx, t - n_ctx)
    tab = pl.BlockSpec((tm, HEAD_DIM), lambda i: (i, 0))
    vec = pl.BlockSpec((1, HEAD_DIM), lambda i: (0, 0))
    return pl.pallas_call(
        _qkv_kernel,
        out_shape=(jax.ShapeDtypeStruct((t, qw), BF16), jax.ShapeDtypeStruct((t, kw), BF16),
                   jax.ShapeDtypeStruct((t, kw), BF16)),
        grid=(t // tm,),
        in_specs=[pl.BlockSpec((tm, qw), lambda i: (i, 0)),
                  pl.BlockSpec((tm, kw), lambda i: (i, qw // kw)),
                  pl.BlockSpec((tm, kw), lambda i: (i, qw // kw + 1)),
                  tab, tab, tab, vec, vec],
        out_specs=(pl.BlockSpec((tm, qw), lambda i: (i, 0)), pl.BlockSpec((tm, kw), lambda i: (i, 0)),
                   pl.BlockSpec((tm, kw), lambda i: (i, 0))),
        compiler_params=_cparams("parallel"),
        name="qkv_prepare",
    )(proj, proj, proj, cos, sa, sb, q_g.reshape(1, HEAD_DIM), k_g.reshape(1, HEAD_DIM))


def _attn_kernel(sink_ref, q_ref, kp_ref, kc_ref, kn_ref, kx_ref, vp_ref, vc_ref, vn_ref, vx_ref, o_ref,
                 *, ctx_blocks, n_blocks):
    b = pl.program_id(0)
    blk = ATT_BLOCK
    rows = Q_PER_KV * blk
    n_ctx = ctx_blocks * blk
    latent = b >= ctx_blocks
    lat_f = jnp.where(latent, 1.0, 0.0)
    prev_f = jnp.where(jnp.logical_and(latent, b - 1 >= ctx_blocks), 1.0, 0.0)
    next_f = jnp.where(jnp.logical_and(latent, b + 1 <= n_blocks - 1), 1.0, 0.0)
    qi = lax.broadcasted_iota(jnp.int32, (rows, 3 * blk + n_ctx), 0) % blk
    col = lax.broadcasted_iota(jnp.int32, (rows, 3 * blk + n_ctx), 1)
    kj = col % blk
    allowed = jnp.where(col < blk, jnp.where(kj >= qi, prev_f, 0.0),
                        jnp.where(col < 2 * blk, lat_f,
                                  jnp.where(col < 3 * blk, jnp.where(kj <= qi, next_f, 0.0), 1.0))) > 0.5
    for h in range(KV_HEADS):
        hs = slice(h * HEAD_DIM, (h + 1) * HEAD_DIM)
        q4 = jnp.concatenate([q_ref[:, (Q_PER_KV * h + g) * HEAD_DIM:(Q_PER_KV * h + g + 1) * HEAD_DIM]
                              for g in range(Q_PER_KV)], axis=0)
        kk = jnp.concatenate([kp_ref[:, hs], kc_ref[:, hs], kn_ref[:, hs], kx_ref[:, hs]], axis=0)
        vv = jnp.concatenate([vp_ref[:, hs], vc_ref[:, hs], vn_ref[:, hs], vx_ref[:, hs]], axis=0)
        s = lax.dot_general(q4, kk, (((1,), (1,)), ((), ())), preferred_element_type=F32)
        s = jnp.where(allowed, s, NEG_INF)
        sk = jnp.concatenate([jnp.full((blk, 1), sink_ref[Q_PER_KV * h + g], F32) for g in range(Q_PER_KV)],
                             axis=0)
        m = jnp.maximum(jnp.max(s, axis=-1, keepdims=True), sk)
        p = jnp.exp(s - m)
        den = jnp.sum(p, axis=-1, keepdims=True) + jnp.exp(sk - m)
        o = jnp.dot(p.astype(BF16), vv, preferred_element_type=F32) / den
        for g in range(Q_PER_KV):
            o_ref[:, (Q_PER_KV * h + g) * HEAD_DIM:(Q_PER_KV * h + g + 1) * HEAD_DIM] = (
                o[g * blk:(g + 1) * blk, :].astype(BF16))


def _attention(q, k, v, sink, n_ctx):
    t = q.shape[0]
    blk = ATT_BLOCK
    nb = t // blk
    cb = n_ctx // blk
    kw = k.shape[1]
    prev = pl.BlockSpec((blk, kw), lambda b, s: (jnp.maximum(b - 1, 0), 0))
    cur = pl.BlockSpec((blk, kw), lambda b, s: (b, 0))
    nxt = pl.BlockSpec((blk, kw), lambda b, s: (jnp.minimum(b + 1, nb - 1), 0))
    ctx = pl.BlockSpec((n_ctx, kw), lambda b, s: (0, 0))
    return pl.pallas_call(
        functools.partial(_attn_kernel, ctx_blocks=cb, n_blocks=nb),
        out_shape=jax.ShapeDtypeStruct(q.shape, BF16),
        grid_spec=pltpu.PrefetchScalarGridSpec(
            num_scalar_prefetch=1, grid=(nb,),
            in_specs=[pl.BlockSpec((blk, q.shape[1]), lambda b, s: (b, 0)),
                      prev, cur, nxt, ctx, prev, cur, nxt, ctx],
            out_specs=pl.BlockSpec((blk, q.shape[1]), lambda b, s: (b, 0))),
        compiler_params=_cparams("parallel"),
        name="window_attention",
    )(sink, q, k, k, k, k, v, v, v, v)


def _pool_kernel(*refs, n_ctx, n_lat, n_tiles):
    ng = len(POOL_WINDOWS)
    cur_refs, prev_refs, next_refs = refs[:ng], refs[ng:2 * ng], refs[2 * ng:3 * ng]
    w_ref, scale_ref, o_ref, xe_ref = refs[3 * ng:]
    i = pl.program_id(0)
    tm = ROW_TILE
    hl = POOL_HALO
    gw = w_ref.shape[1]
    prev_valid = i >= 2
    next_valid = jnp.logical_and(i >= 1, i < n_tiles - 1)
    pos = lax.broadcasted_iota(jnp.int32, (tm, 1), 0) + jnp.where(i == 0, 0, (i - 1) * tm)
    seg_len = jnp.where(i == 0, n_ctx, n_lat)
    for g, win in enumerate(POOL_WINDOWS):
        cur = cur_refs[g][...]
        xe_ref[0:hl, :] = jnp.where(prev_valid, prev_refs[g][...], 0.0)
        xe_ref[hl:hl + tm, :] = cur
        xe_ref[hl + tm:hl + tm + hl, :] = jnp.where(next_valid, next_refs[g][...], 0.0)
        acc = xe_ref[hl - win // 2:hl - win // 2 + tm, :]
        for o in range(1 - win // 2, win // 2):
            acc = acc + xe_ref[hl + o:hl + o + tm, :]
        cnt = jnp.minimum(pos - win // 2 + win, seg_len) - jnp.maximum(pos - win // 2, 0)
        d = acc / cnt.astype(F32) - cur
        y = jnp.dot(d.astype(BF16), w_ref[g].astype(BF16), preferred_element_type=F32)
        o_ref[:, g * gw:(g + 1) * gw] = (y * scale_ref[:, g * gw:(g + 1) * gw]).astype(BF16)


def _pool_mixer(proj, col0, w_pool, pool_scale, n_ctx):
    t = proj.shape[0]
    tm, hl = ROW_TILE, POOL_HALO
    ng, gw = w_pool.shape[0], w_pool.shape[1]
    nt = t // tm
    cb = col0 // gw
    per = tm // hl
    cur = [pl.BlockSpec((tm, gw), functools.partial(lambda i, g: (i, cb + g), g=g)) for g in range(ng)]
    prev = [pl.BlockSpec((hl, gw), functools.partial(lambda i, g: (jnp.maximum(i * per - 1, 0), cb + g), g=g))
            for g in range(ng)]
    nxt = [pl.BlockSpec((hl, gw), functools.partial(lambda i, g: (jnp.minimum((i + 1) * per, nt * per - 1), cb + g), g=g))
           for g in range(ng)]
    return pl.pallas_call(
        functools.partial(_pool_kernel, n_ctx=n_ctx, n_lat=t - n_ctx, n_tiles=nt),
        out_shape=jax.ShapeDtypeStruct((t, ng * gw), BF16),
        grid=(nt,),
        in_specs=cur + prev + nxt + [pl.BlockSpec((ng, gw, gw), lambda i: (0, 0, 0)),
                                     pl.BlockSpec((1, ng * gw), lambda i: (0, 0))],
        out_specs=pl.BlockSpec((tm, ng * gw), lambda i: (i, 0)),
        scratch_shapes=[pltpu.VMEM((tm + 2 * hl, gw), F32)],
        compiler_params=_cparams("parallel"),
        name="pool_mixer",
    )(*([proj] * (3 * ng)), w_pool, pool_scale.reshape(1, ng * gw))


def _s5_powers(lr, li, dt, kcol):
    mag = jnp.exp(kcol * (lr * dt))
    ang = kcol * (li * dt)
    return mag * jnp.cos(ang), mag * jnp.sin(ang)


def _s5_expand(z_re, z_im, e_re, e_im):
    n, p = e_re.shape
    hh = z_re.shape[0]
    er = jnp.broadcast_to(e_re[:, None, :], (n, hh, p)).reshape(n * hh, p)
    ei = jnp.broadcast_to(e_im[:, None, :], (n, hh, p)).reshape(n * hh, p)
    zr = jnp.broadcast_to(z_re[None, :, :], (n, hh, p)).reshape(n * hh, p)
    zi = jnp.broadcast_to(z_im[None, :, :], (n, hh, p)).reshape(n * hh, p)
    return zr * er - zi * ei, zr * ei + zi * er


def _s5_discretize_t(lr, li, dt, bt_re, bt_im):
    mag = jnp.exp(lr * dt)
    a_re, a_im = mag * jnp.cos(li * dt), mag * jnp.sin(li * dt)
    den = lr * lr + li * li
    n_re = a_re - 1.0
    k_re = (n_re * lr + a_im * li) / den
    k_im = (a_im * lr - n_re * li) / den
    return k_re * bt_re - k_im * bt_im, k_re * bt_im + k_im * bt_re


def _s5_group_params(lam_re_ref, lam_im_ref, ldt_ref, bt_re_ref, bt_im_ref, d):
    lr, li = lam_re_ref[d], lam_im_ref[d]
    dt = jnp.exp(ldt_ref[d])
    bb_re, bb_im = _s5_discretize_t(lr, li, dt, bt_re_ref[d], bt_im_ref[d])
    return lr, li, dt, bb_re, bb_im


def _s5_state_kernel(u_ref, lam_re_ref, lam_im_ref, ldt_ref, bt_re_ref, bt_im_ref, sf_ref, sb_ref):
    lc = S5_CHUNK
    u = u_ref[...].astype(BF16)
    t_col = lax.broadcasted_iota(jnp.int32, (lc, 1), 0).astype(F32)
    for d, out_ref, kcol in ((0, sf_ref, (lc - 1) - t_col), (1, sb_ref, t_col)):
        lr, li, dt, bb_re, bb_im = _s5_group_params(lam_re_ref, lam_im_ref, ldt_ref, bt_re_ref, bt_im_ref, d)
        e_re, e_im = _s5_powers(lr, li, dt, kcol)
        w_re, w_im = _s5_expand(bb_re, bb_im, e_re, e_im)
        w = jnp.concatenate([w_re, w_im], axis=1).astype(BF16)
        out_ref[...] = jnp.dot(u, w, preferred_element_type=F32)


def _s5_carry_kernel(sf_ref, sb_ref, lam_re_ref, lam_im_ref, ldt_ref, hf_ref, hb_ref, sw_ref, *, ctx_chunks):
    nc = sf_ref.shape[0]
    lanes = sf_ref.shape[1]
    first = (lax.broadcasted_iota(jnp.int32, (1, lanes), 1) % (2 * S5_P)) < S5_P

    def scan(d, s_ref, h_ref, order):
        lr, li, dt = lam_re_ref[d], lam_im_ref[d], jnp.exp(ldt_ref[d])
        mag = jnp.exp(S5_CHUNK * (lr * dt))
        ang = S5_CHUNK * (li * dt)
        pa = mag * jnp.cos(ang)
        a_im = mag * jnp.sin(ang)
        qa = jnp.where(first, -a_im, a_im)
        sw_ref[...] = pltpu.roll(s_ref[...], S5_P, 1)
        zero = jnp.zeros((1, lanes), F32)

        def run(lo, n, step, carry):
            def body(k, hc):
                h, hs = hc
                c = lo + k * step
                h_ref[pl.ds(c, 1), :] = h
                s = s_ref[pl.ds(c, 1), :]
                ss = sw_ref[pl.ds(c, 1), :]
                return pa * h + qa * hs + s, pa * hs - qa * h + ss
            return lax.fori_loop(0, n, body, carry)

        carry = (zero, zero)
        for lo, n, step in order:
            carry = run(lo, n, step, carry)

    scan(0, sf_ref, hf_ref, ((0, nc, 1),))
    scan(1, sb_ref, hb_ref, ((ctx_chunks - 1, ctx_chunks, -1), (nc - 1, nc - ctx_chunks, -1)))


def _s5_output_kernel(u_ref, hf_ref, hb_ref, lam_re_ref, lam_im_ref, ldt_ref, bt_re_ref, bt_im_ref,
                      c_re_ref, c_im_ref, y_ref):
    lc = S5_CHUNK
    half = lc // 2
    n = lc * S5_H
    u = u_ref[...].astype(BF16)
    t_col = lax.broadcasted_iota(jnp.int32, (lc, 1), 0).astype(F32)
    s_idx = lax.broadcasted_iota(jnp.int32, (n, n), 0) // S5_H
    t_idx = lax.broadcasted_iota(jnp.int32, (n, n), 1) // S5_H
    mt = jnp.zeros((n, n), F32)
    y = jnp.zeros((u.shape[0], n), F32)
    for d, h_ref in ((0, hf_ref), (1, hb_ref)):
        lr, li, dt, bb_re, bb_im = _s5_group_params(lam_re_ref, lam_im_ref, ldt_ref, bt_re_ref, bt_im_ref, d)
        c_re, c_im = c_re_ref[d], c_im_ref[d]
        if d == 0:
            kl, kr, kc = t_col - half, half - t_col, t_col + 1.0
            keep = s_idx <= t_idx
        else:
            kl, kr, kc = half - t_col, t_col - half, lc - t_col
            keep = s_idx >= t_idx
        l_re, l_im = _s5_expand(c_re, c_im, *_s5_powers(lr, li, dt, kl))
        r_re, r_im = _s5_expand(bb_re, bb_im, *_s5_powers(lr, li, dt, kr))
        lst = jnp.concatenate([l_re, -l_im], axis=1)
        rst = jnp.concatenate([r_re, r_im], axis=1)
        r3, l3 = _split3(rst), _split3(lst)
        m = jnp.zeros((n, n), F32)
        for ra, la in ((0, 0), (0, 1), (1, 0), (0, 2), (1, 1), (2, 0)):
            m = m + lax.dot_general(r3[ra], l3[la], (((1,), (1,)), ((), ())), preferred_element_type=F32)
        mt = mt + jnp.where(keep, m, 0.0)
        ca_re, ca_im = _s5_expand(c_re, c_im, *_s5_powers(lr, li, dt, kc))
        t2 = jnp.concatenate([ca_re, -ca_im], axis=1).astype(BF16)
        y = y + lax.dot_general(h_ref[...].astype(BF16), t2, (((1,), (1,)), ((), ())),
                                preferred_element_type=F32)
    y_ref[...] = y + jnp.dot(u, mt.astype(BF16), preferred_element_type=F32)


def _s5_scan(s_all, n_ctx, lam_re, lam_im, log_dt, b_re, b_im, c_re, c_im):
    t, cw = s_all.shape
    g, p, h, lc = cw // S5_H, S5_P, S5_H, S5_CHUNK
    nc = t // lc
    ug = s_all.reshape(nc, lc, g, h).transpose(2, 0, 1, 3).reshape(g, nc, lc * h)
    lam_re4 = lam_re.transpose(1, 0, 2)[:, :, None, :]
    lam_im4 = lam_im.transpose(1, 0, 2)[:, :, None, :]
    ldt4 = jnp.broadcast_to(log_dt.T[:, :, None, None], (g, 2, 1, p))
    bt_re = b_re.transpose(1, 0, 3, 2)
    bt_im = b_im.transpose(1, 0, 3, 2)
    c_re4 = c_re.transpose(1, 0, 2, 3)
    c_im4 = c_im.transpose(1, 0, 2, 3)
    vec4 = pl.BlockSpec((None, 2, 1, p), lambda i: (i, 0, 0, 0))
    mat4 = pl.BlockSpec((None, 2, h, p), lambda i: (i, 0, 0, 0))
    u_spec = pl.BlockSpec((None, nc, lc * h), lambda i: (i, 0, 0))
    st_spec = pl.BlockSpec((nc, 2 * p), lambda i: (0, i))
    st_shape = jax.ShapeDtypeStruct((nc, g * 2 * p), F32)
    s_f, s_b = pl.pallas_call(
        _s5_state_kernel, out_shape=(st_shape, st_shape), grid=(g,),
        in_specs=[u_spec, vec4, vec4, vec4, mat4, mat4],
        out_specs=(st_spec, st_spec),
        compiler_params=_cparams("parallel"), name="s5_chunk_states",
    )(ug, lam_re4, lam_im4, ldt4, bt_re, bt_im)

    def lanes(x):
        return jnp.concatenate([x, x], axis=-1).reshape(2, 1, g * 2 * p)
    cols = 8 * 2 * p
    row = pl.BlockSpec((2, 1, cols), lambda j: (0, 0, j))
    blk = pl.BlockSpec((nc, cols), lambda j: (0, j))
    h_f, h_b = pl.pallas_call(
        functools.partial(_s5_carry_kernel, ctx_chunks=n_ctx // lc),
        out_shape=(st_shape, st_shape), grid=(g * 2 * p // cols,),
        in_specs=[blk, blk, row, row, row], out_specs=(blk, blk),
        scratch_shapes=[pltpu.VMEM((nc, cols), F32)],
        compiler_params=_cparams("parallel"), name="s5_chunk_carry",
    )(s_f, s_b, lanes(lam_re), lanes(lam_im), lanes(jnp.broadcast_to(log_dt[:, :, None], (2, g, p))))

    yg = pl.pallas_call(
        _s5_output_kernel, out_shape=jax.ShapeDtypeStruct((g, nc, lc * h), F32), grid=(g,),
        in_specs=[u_spec, st_spec, st_spec, vec4, vec4, vec4, mat4, mat4, mat4, mat4],
        out_specs=u_spec,
        compiler_params=_cparams("parallel"), name="s5_outputs",
    )(ug, h_f, h_b, lam_re4, lam_im4, ldt4, bt_re, bt_im, c_re4, c_im4)
    return yg.reshape(g, nc, lc, h).transpose(1, 2, 0, 3).reshape(t, cw)


def _glu_kernel(y_ref, u_ref, dk_ref, w_ref, b_ref, o_ref, wbf_ref):
    @pl.when(pl.program_id(0) == 0)
    def _():
        wbf_ref[...] = w_ref[...].astype(BF16)

    yl = dk_ref[...] * u_ref[...] + y_ref[...]
    gl = _gelu(yl)
    z = jnp.dot(gl.astype(BF16), wbf_ref[...], preferred_element_type=F32) + b_ref[...]
    o_ref[...] = (gl * _sigmoid(z)).astype(BF16)


def _s5_glu(y_scan, proj, d_skip, glu_w, glu_b, n_ctx):
    t, cw = y_scan.shape
    tm = ROW_TILE
    off = n_ctx // tm
    n_lat = t - n_ctx
    vec = pl.BlockSpec((1, cw), lambda i: (0, 0))
    return pl.pallas_call(
        _glu_kernel, out_shape=jax.ShapeDtypeStruct((n_lat, cw), BF16), grid=(n_lat // tm,),
        in_specs=[pl.BlockSpec((tm, cw), lambda i: (i + off, 0)),
                  pl.BlockSpec((tm, cw), lambda i: (i + off, 0)),
                  vec, pl.BlockSpec((cw, cw), lambda i: (0, 0)), vec],
        out_specs=pl.BlockSpec((tm, cw), lambda i: (i, 0)),
        scratch_shapes=[pltpu.VMEM((cw, cw), BF16)],
        compiler_params=_cparams("arbitrary"), name="s5_glu",
    )(y_scan, proj, d_skip.reshape(1, cw), glu_w, glu_b.reshape(1, cw))


def _gmlp_kernel(*refs, n_col):
    u_refs, v_refs = refs[:n_col], refs[n_col:2 * n_col]
    lng_ref, lnb_ref, ws_ref, bst_ref, o_ref = refs[2 * n_col:]
    u = _gelu(jnp.concatenate([r[...] for r in u_refs], axis=1))
    v = _gelu(jnp.concatenate([r[...] for r in v_refs], axis=1))
    mu = jnp.mean(v, axis=-1, keepdims=True)
    vc = v - mu
    var = jnp.mean(vc * vc, axis=-1, keepdims=True)
    vn = (vc * lax.rsqrt(var + EPS) * lng_ref[...] + lnb_ref[...]).astype(BF16)
    gw = u.shape[1] // SG_HEADS
    for c in range(u.shape[0] // SG_CHUNK):
        rs = slice(c * SG_CHUNK, (c + 1) * SG_CHUNK)
        for g in range(SG_HEADS):
            cs = slice(g * gw, (g + 1) * gw)
            mixed = jnp.dot(ws_ref[g].astype(BF16), vn[rs, cs], preferred_element_type=F32)
            mixed = mixed + bst_ref[:, g:g + 1]
            o_ref[rs, cs] = (u[rs, cs] * mixed).astype(BF16)


def _gmlp_mixer(proj, col_u, col_v, width, ln_g, ln_b, w_s, b_s, n_ctx):
    t = proj.shape[0]
    tm, cwid = ROW_TILE, 1024
    off = n_ctx // tm
    n_lat = t - n_ctx
    n_col = width // cwid
    specs = [pl.BlockSpec((tm, cwid), functools.partial(lambda i, cb: (i + off, cb), cb=(c0 // cwid) + k))
             for c0 in (col_u, col_v) for k in range(n_col)]
    vec = pl.BlockSpec((1, width), lambda i: (0, 0))
    return pl.pallas_call(
        functools.partial(_gmlp_kernel, n_col=n_col),
        out_shape=jax.ShapeDtypeStruct((n_lat, width), BF16), grid=(n_lat // tm,),
        in_specs=specs + [vec, vec, pl.BlockSpec(w_s.shape, lambda i: (0, 0, 0)),
                          pl.BlockSpec((SG_CHUNK, SG_HEADS), lambda i: (0, 0))],
        out_specs=pl.BlockSpec((tm, width), lambda i: (i, 0)),
        compiler_params=_cparams("parallel"), name="gmlp_mixer",
    )(*([proj] * (2 * n_col)), ln_g.reshape(1, width), ln_b.reshape(1, width), w_s, b_s.T)


def _router_kernel(x_ref, g_ref, sh_ref, sc_ref, wr_ref, br_ref, hp_ref, rt_ref, cnt_ref, w3_ref, run_ref):
    @pl.when(pl.program_id(0) == 0)
    def _():
        hi, mid, lo = _split3(wr_ref[...])
        w3_ref[0], w3_ref[1], w3_ref[2] = hi, mid, lo
        run_ref[...] = jnp.zeros_like(run_ref)

    h = _modulated_norm(x_ref[...], g_ref, sh_ref, sc_ref)
    tm = h.shape[0]
    half = h.shape[1] // 2
    packed = pltpu.pack_elementwise([h[:, :half], h[:, half:]], packed_dtype=jnp.bfloat16)
    slab = half // LANES
    for s in range(slab):
        hp_ref[pl.ds(s, tm, stride=slab), :] = packed[:, s * LANES:(s + 1) * LANES]

    h3 = _split3(h)
    logits = br_ref[...]
    for a, b in ((0, 0), (0, 1), (1, 0), (0, 2), (1, 1), (2, 0)):
        logits = logits + jnp.dot(h3[a], w3_ref[b], preferred_element_type=F32)

    lane = lax.broadcasted_iota(jnp.int32, logits.shape, 1)
    big = jnp.int32(ROUTER_LANES)
    gl = jnp.where(lane < MOE_GROUPS, logits, NEG_INF)
    gm = jnp.max(gl, axis=-1, keepdims=True)
    g_w = 1.0 / jnp.sum(jnp.exp(gl - gm), axis=-1, keepdims=True)
    g_idx = jnp.min(jnp.where(gl == gm, lane, big), axis=-1, keepdims=True)
    lo_lane = MOE_GROUPS + g_idx * MOE_PER_GROUP
    fl = jnp.where(jnp.logical_and(lane >= lo_lane, lane < lo_lane + MOE_PER_GROUP), logits, NEG_INF)
    m1 = jnp.max(fl, axis=-1, keepdims=True)
    i1 = jnp.min(jnp.where(fl == m1, lane, big), axis=-1, keepdims=True)
    fl2 = jnp.where(lane == i1, NEG_INF, fl)
    m2 = jnp.max(fl2, axis=-1, keepdims=True)
    i2 = jnp.min(jnp.where(fl2 == m2, lane, big), axis=-1, keepdims=True)
    e2 = jnp.exp(m2 - m1)
    w1 = g_w / (1.0 + e2)
    w2 = g_w * e2 / (1.0 + e2)
    oh1 = jnp.where(lane == i1 - MOE_GROUPS, 1.0, 0.0)
    oh2 = jnp.where(lane == i2 - MOE_GROUPS, 1.0, 0.0)
    both = oh1 + oh2
    tri = jnp.where(lax.broadcasted_iota(jnp.int32, (tm, tm), 1) < lax.broadcasted_iota(jnp.int32, (tm, tm), 0),
                    1.0, 0.0).astype(BF16)
    before = jnp.dot(tri, both.astype(BF16), preferred_element_type=F32) + run_ref[...]
    r1 = jnp.sum(before * oh1, axis=-1, keepdims=True)
    r2 = jnp.sum(before * oh2, axis=-1, keepdims=True)
    run_ref[...] = run_ref[...] + jnp.sum(both, axis=0, keepdims=True)
    cnt_ref[...] = jnp.broadcast_to(run_ref[...], cnt_ref.shape)
    rt_ref[...] = jnp.where(lane == 0, (i1 - MOE_GROUPS).astype(F32),
                            jnp.where(lane == 1, (i2 - MOE_GROUPS).astype(F32),
                                      jnp.where(lane == 2, w1,
                                                jnp.where(lane == 3, w2,
                                                          jnp.where(lane == 4, r1, jnp.where(lane == 5, r2, 0.0))))))


def _moe_router(xt, tile_off, n_rows, g, shifts, scales, first_latent_tile, w_grp, b_grp, w_rt, b_rt):
    d = xt.shape[1]
    tm = ROW_TILE
    slab = d // 2 // LANES
    ne = MOE_GROUPS * MOE_PER_GROUP
    wr = jnp.concatenate([w_grp, w_rt.transpose(1, 0, 2).reshape(d, ne),
                          jnp.zeros((d, ROUTER_LANES - MOE_GROUPS - ne), F32)], axis=1)
    br = jnp.concatenate([b_grp, b_rt.reshape(ne), jnp.zeros((ROUTER_LANES - MOE_GROUPS - ne,), F32)])
    sel = lambda i: (jnp.where(i + tile_off >= first_latent_tile, 1, 0), 0, 0)
    return pl.pallas_call(
        _router_kernel,
        out_shape=(jax.ShapeDtypeStruct((n_rows * slab, LANES), jnp.uint32),
                   jax.ShapeDtypeStruct((n_rows, ROUTER_LANES), F32),
                   jax.ShapeDtypeStruct((8, ROUTER_LANES), F32)),
        grid=(n_rows // tm,),
        in_specs=[pl.BlockSpec((tm, d), lambda i: (i + tile_off, 0)),
                  pl.BlockSpec((1, d), lambda i: (0, 0)),
                  pl.BlockSpec((None, 1, d), sel), pl.BlockSpec((None, 1, d), sel),
                  pl.BlockSpec((d, ROUTER_LANES), lambda i: (0, 0)),
                  pl.BlockSpec((1, ROUTER_LANES), lambda i: (0, 0))],
        out_specs=(pl.BlockSpec((tm * slab, LANES), lambda i: (i, 0)),
                   pl.BlockSpec((tm, ROUTER_LANES), lambda i: (i, 0)),
                   pl.BlockSpec((8, ROUTER_LANES), lambda i: (0, 0))),
        scratch_shapes=[pltpu.VMEM((3, d, ROUTER_LANES), BF16), pltpu.VMEM((1, ROUTER_LANES), F32)],
        compiler_params=_cparams("arbitrary"), name="moe_router",
    )(xt, g.reshape(1, d), shifts, scales, wr, br.reshape(1, ROUTER_LANES))


def _row_gather_kernel(nu_ref, idx_ref, src_hbm, out_ref, sem, *, slab):
    t = pl.program_id(0)
    tm = out_ref.shape[0] // slab

    @pl.when(t < nu_ref[0])
    def _():
        def issue(r, carry):
            src = pl.multiple_of(idx_ref[t * tm + r] * slab, slab)
            dst = pl.multiple_of(r * slab, slab)
            pltpu.make_async_copy(src_hbm.at[pl.ds(src, slab)], out_ref.at[pl.ds(dst, slab)], sem).start()
            return carry

        lax.fori_loop(0, tm, issue, 0, unroll=DMA_ISSUE_UNROLL)
        pltpu.make_async_copy(src_hbm.at[pl.ds(0, tm * slab)], out_ref, sem).wait()


def _moe_dispatch(hp, slab, src_tok, n_used, n_tiles):
    tm = MOE_TILE
    last = lambda t, nu, idx: (jnp.minimum(t, nu[0] - 1), 0)
    return pl.pallas_call(
        functools.partial(_row_gather_kernel, slab=slab),
        out_shape=jax.ShapeDtypeStruct((n_tiles * tm * slab, LANES), hp.dtype),
        grid_spec=pltpu.PrefetchScalarGridSpec(
            num_scalar_prefetch=2, grid=(n_tiles,),
            in_specs=[pl.BlockSpec(memory_space=pl.ANY)],
            out_specs=pl.BlockSpec((tm * slab, LANES), last),
            scratch_shapes=[pltpu.SemaphoreType.DMA(())]),
        compiler_params=_cparams("arbitrary", unchecked_dma=True), name="moe_dispatch",
    )(n_used, src_tok, hp)


def _unpack_pairs(xp):
    lo = pltpu.unpack_elementwise(xp, index=0, packed_dtype=jnp.bfloat16, unpacked_dtype=F32)
    hi = pltpu.unpack_elementwise(xp, index=1, packed_dtype=jnp.bfloat16, unpacked_dtype=F32)
    return lo.astype(BF16), hi.astype(BF16)


def _expert_up_kernel(te_ref, nu_ref, xs_ref, wg_ref, wu_ref, h_ref, wgb_ref, wub_ref):
    t = pl.program_id(0)
    tc = jnp.minimum(t, nu_ref[0] - 1)
    new_expert = jnp.logical_or(t == 0, te_ref[tc] != te_ref[jnp.maximum(tc - 1, 0)])

    @pl.when(t < nu_ref[0])
    def _():
        @pl.when(new_expert)
        def _():
            wgb_ref[...] = wg_ref[...].astype(BF16)
            wub_ref[...] = wu_ref[...].astype(BF16)

        tm = h_ref.shape[0]
        slab = xs_ref.shape[0] // tm
        parts = [_unpack_pairs(xs_ref[pl.ds(s, tm, stride=slab), :]) for s in range(slab)]
        lo = jnp.concatenate([p[0] for p in parts], axis=1)
        hi = jnp.concatenate([p[1] for p in parts], axis=1)
        half = lo.shape[1]
        gate = (jnp.dot(lo, wgb_ref[:half, :], preferred_element_type=F32)
                + jnp.dot(hi, wgb_ref[half:, :], preferred_element_type=F32))
        up = (jnp.dot(lo, wub_ref[:half, :], preferred_element_type=F32)
              + jnp.dot(hi, wub_ref[half:, :], preferred_element_type=F32))
        h_ref[...] = (gate * _sigmoid(gate) * up).astype(BF16)


def _expert_down_kernel(te_ref, nu_ref, h_ref, wd_ref, y_ref, wdb_ref):
    t = pl.program_id(0)
    tc = jnp.minimum(t, nu_ref[0] - 1)
    new_expert = jnp.logical_or(t == 0, te_ref[tc] != te_ref[jnp.maximum(tc - 1, 0)])

    @pl.when(t < nu_ref[0])
    def _():
        @pl.when(new_expert)
        def _():
            wdb_ref[...] = wd_ref[...].astype(BF16)

        y = jnp.dot(h_ref[...], wdb_ref[...], preferred_element_type=F32)
        tm = h_ref.shape[0]
        slab = y.shape[1] // LANES
        for s in range(slab):
            y_ref[pl.ds(s, tm, stride=slab), :] = y[:, s * LANES:(s + 1) * LANES]


def _moe_experts(xs, tile_expert, n_used, layer, w_gate, w_up, w_down):
    tm = MOE_TILE
    d, ff = w_gate.shape[2], w_gate.shape[3]
    in_slab, out_slab = d // 2 // LANES, d // LANES
    npad = xs.shape[0] // in_slab
    nt = npad // tm
    tile = lambda t, nu: jnp.minimum(t, nu[0] - 1)
    hmid = pl.pallas_call(
        _expert_up_kernel,
        out_shape=jax.ShapeDtypeStruct((npad, ff), BF16),
        grid_spec=pltpu.PrefetchScalarGridSpec(
            num_scalar_prefetch=2, grid=(nt,),
            in_specs=[pl.BlockSpec((tm * in_slab, LANES), lambda t, te, nu: (tile(t, nu), 0)),
                      pl.BlockSpec((None, None, d, ff), lambda t, te, nu: (layer, te[tile(t, nu)], 0, 0)),
                      pl.BlockSpec((None, None, d, ff), lambda t, te, nu: (layer, te[tile(t, nu)], 0, 0))],
            out_specs=pl.BlockSpec((tm, ff), lambda t, te, nu: (tile(t, nu), 0)),
            scratch_shapes=[pltpu.VMEM((d, ff), BF16), pltpu.VMEM((d, ff), BF16)]),
        compiler_params=_cparams("arbitrary"), name="moe_expert_up",
    )(tile_expert, n_used, xs, w_gate, w_up)
    return pl.pallas_call(
        _expert_down_kernel,
        out_shape=jax.ShapeDtypeStruct((npad * out_slab, LANES), F32),
        grid_spec=pltpu.PrefetchScalarGridSpec(
            num_scalar_prefetch=2, grid=(nt,),
            in_specs=[pl.BlockSpec((tm, ff), lambda t, te, nu: (tile(t, nu), 0)),
                      pl.BlockSpec((None, None, ff, d), lambda t, te, nu: (layer, te[tile(t, nu)], 0, 0))],
            out_specs=pl.BlockSpec((tm * out_slab, LANES), lambda t, te, nu: (tile(t, nu), 0)),
            scratch_shapes=[pltpu.VMEM((ff, d), BF16)]),
        compiler_params=_cparams("arbitrary"), name="moe_expert_down",
    )(tile_expert, n_used, hmid, w_down)


def _combine_kernel(pos_ref, y_hbm, rt_ref, res_ref, gate_ref, o_ref, buf_ref, sem):
    i = pl.program_id(0)
    tm = o_ref.shape[0]
    slab = o_ref.shape[1] // LANES

    def issue(r, carry):
        dst = pl.multiple_of(r * slab, slab)
        for k in range(2):
            src = pl.multiple_of(pos_ref[2 * (i * tm + r) + k] * slab, slab)
            pltpu.make_async_copy(y_hbm.at[pl.ds(src, slab)], buf_ref.at[k, pl.ds(dst, slab)], sem).start()
        return carry

    lax.fori_loop(0, tm, issue, 0, unroll=DMA_ISSUE_UNROLL)
    pltpu.make_async_copy(y_hbm.at[pl.ds(0, 2 * tm * slab)], buf_ref.reshape(2 * tm * slab, LANES), sem).wait()
    w1, w2 = rt_ref[:, 2:3], rt_ref[:, 3:4]
    for s in range(slab):
        cs = slice(s * LANES, (s + 1) * LANES)
        mix = w1 * buf_ref[0, pl.ds(s, tm, stride=slab), :] + w2 * buf_ref[1, pl.ds(s, tm, stride=slab), :]
        o_ref[:, cs] = res_ref[:, cs] + gate_ref[:, cs] * mix


def _moe_combine(y_slabs, pos, route, res, res_tile_off, n_rows, gates, first_latent_tile):
    d = res.shape[1]
    tm = ROW_TILE
    slab = d // LANES
    sel = lambda i, p: (jnp.where(i + res_tile_off >= first_latent_tile, 1, 0), 0, 0)
    return pl.pallas_call(
        _combine_kernel,
        out_shape=jax.ShapeDtypeStruct((n_rows, d), F32),
        grid_spec=pltpu.PrefetchScalarGridSpec(
            num_scalar_prefetch=1, grid=(n_rows // tm,),
            in_specs=[pl.BlockSpec(memory_space=pl.ANY),
                      pl.BlockSpec((tm, ROUTER_LANES), lambda i, p: (i, 0)),
                      pl.BlockSpec((tm, d), lambda i, p: (i + res_tile_off, 0)),
                      pl.BlockSpec((None, 1, d), sel)],
            out_specs=pl.BlockSpec((tm, d), lambda i, p: (i, 0)),
            scratch_shapes=[pltpu.VMEM((2, tm * slab, LANES), F32), pltpu.SemaphoreType.DMA(())]),
        compiler_params=_cparams("arbitrary", unchecked_dma=True), name="moe_combine",
    )(pos.reshape(-1), y_slabs, route, res, gates)


def _routing_plan(route, counts_row, n_tiles):
    tm = MOE_TILE
    n = route.shape[0]
    ids = route[:, 0:2].astype(jnp.int32)
    rank = route[:, 4:6].astype(jnp.int32)
    counts = counts_row[:MOE_EXPERTS].astype(jnp.int32)
    tiles = (counts + tm - 1) // tm
    tile_end = jnp.cumsum(tiles)
    tile_start = tile_end - tiles
    onehot = ids[:, :, None] == jnp.arange(MOE_EXPERTS)[None, None, :]
    pos = jnp.sum(jnp.where(onehot, tile_start[None, None, :] * tm, 0), axis=-1) + rank
    pos = pos.reshape(-1)
    tok = jnp.arange(2 * n, dtype=jnp.int32) // 2
    src_tok = jnp.zeros((n_tiles * tm,), jnp.int32).at[pos].set(tok)
    n_used = tile_end[-1].astype(jnp.int32).reshape(1)
    tile_ids = jnp.arange(n_tiles, dtype=jnp.int32)
    tile_expert = jnp.minimum(jnp.sum((tile_end[None, :] <= tile_ids[:, None]).astype(jnp.int32), axis=1),
                              MOE_EXPERTS - 1)
    return pos.astype(jnp.int32), src_tok, tile_expert, n_used


def _hier_moe(xt, tile_off, n_rows, norm_g, shifts, scales, gates, first_latent_tile, layer, moe):
    w_grp, b_grp, w_rt, b_rt, w_gate, w_up, w_down = moe
    hp, route, counts = _moe_router(xt, tile_off, n_rows, norm_g, shifts, scales, first_latent_tile,
                                    w_grp, b_grp, w_rt, b_rt)
    n_tiles = (2 * n_rows) // MOE_TILE + MOE_EXPERTS
    pos, src_tok, tile_expert, n_used = _routing_plan(route, counts[0], n_tiles)
    xs = _moe_dispatch(hp, xt.shape[1] // 2 // LANES, src_tok, n_used, n_tiles)
    y_slabs = _moe_experts(xs, tile_expert, n_used, layer, w_gate, w_up, w_down)
    return _moe_combine(y_slabs, pos, route, xt, tile_off, n_rows, gates, first_latent_tile)


def kernel(x, c, ctx, c_ctx, mod_w, mod_b, norm1_g, norm2_g, ab_w_in, ab_q_norm, ab_k_norm, ab_sink, ab_pool_w, ab_pool_scale, ab_w_out, cd_w_in, s5_lam_re, s5_lam_im, s5_log_dt, s5_b_re, s5_b_im, s5_c_re, s5_c_im, s5_d, s5_glu_w, s5_glu_b, sg_ln_g, sg_ln_b, sg_w, sg_b, cd_w_out, moe_w_group, moe_b_group, moe_w_router, moe_b_router, moe_w_gate, moe_w_up, moe_w_down):
    batch, n_lat, d = x.shape
    n_ctx = ctx.shape[1]
    assert batch == 1 and n_ctx == ROW_TILE and n_lat % ROW_TILE == 0 and mod_w.shape[0] == 2
    ctx_tiles = n_ctx // ROW_TILE

    cvecs = jnp.concatenate([c_ctx[None, :], c, jnp.zeros((6, d), F32)], axis=0)
    mod = _modulation(cvecs, mod_w, mod_b)[:, 0:2, :].reshape(2, 2, 6, 1, d)
    xt = jnp.concatenate([ctx[0], x[0]], axis=0)

    def moe_params(layer):
        return (moe_w_group[layer], moe_b_group[layer], moe_w_router[layer], moe_b_router[layer],
                moe_w_gate, moe_w_up, moe_w_down)

    m0 = mod[0]
    sh1, sc1, g1, sh2, sc2, g2 = (m0[:, k] for k in range(6))
    h = _norm_modulate(xt, norm1_g[0], sh1, sc1)
    proj = _in_projection(h, ab_w_in[0])
    q, k, v = _qkv_prepare(proj, ab_q_norm[0], ab_k_norm[0], n_ctx)
    att = _attention(q, k, v, ab_sink[0], n_ctx)
    pooled = _pool_mixer(proj, Q_HEADS * HEAD_DIM + 2 * KV_HEADS * HEAD_DIM, ab_pool_w[0], ab_pool_scale[0], n_ctx)
    xt = _out_projection(att, pooled, ab_w_out[0], xt, g1, 0, lambda i: jnp.minimum(i, 1))
    xt = _hier_moe(xt, 0, n_ctx + n_lat, norm2_g[0], sh2, sc2, g2, ctx_tiles, 0, moe_params(0))

    m1 = mod[1]
    sh1, sc1, g1, sh2, sc2, g2 = (m1[:, k] for k in range(6))
    h = _norm_modulate(xt, norm1_g[1], sh1, sc1)
    proj = _in_projection(h, cd_w_in[0])
    cw = s5_d.shape[1]
    dw = sg_ln_g.shape[1]
    y_scan = _s5_scan(proj[:, :cw], n_ctx, s5_lam_re[0], s5_lam_im[0], s5_log_dt[0], s5_b_re[0], s5_b_im[0],
                      s5_c_re[0], s5_c_im[0])
    y_c = _s5_glu(y_scan, proj, s5_d[0], s5_glu_w[0], s5_glu_b[0], n_ctx)
    y_d = _gmlp_mixer(proj, cw, cw + dw, dw, sg_ln_g[0], sg_ln_b[0], sg_w[0], sg_b[0], n_ctx)
    xl = _out_projection(y_c, y_d, cd_w_out[0], xt, g1, ctx_tiles, lambda i: 1)
    out = _hier_moe(xl, 0, n_lat, norm2_g[1], sh2, sc2, g2, 0, 1, moe_params(1))
    return out[None]
```

```python
import functools
import math

import jax
import jax.numpy as jnp
from jax import lax
from jax.experimental import pallas as pl
from jax.experimental.pallas import tpu as pltpu

F32 = jnp.float32
BF16 = jnp.bfloat16

EPS = 1e-6
NEG_INF = -1e30
GRID_W = 64
HEAD_DIM = 128
Q_HEADS = 16
KV_HEADS = 4
Q_PER_KV = Q_HEADS // KV_HEADS
ATT_BLOCK = 128
ROPE_BASE = 10000.0
ROPE_PAIRS = HEAD_DIM // 4
POOL_WINDOWS = (2, 4, 8, 16)
POOL_HALO = 8
S5_H = 16
S5_P = 64
S5_CHUNK = 32
SG_HEADS = 8
SG_CHUNK = 128
MOE_GROUPS = 4
MOE_PER_GROUP = 8
MOE_EXPERTS = MOE_GROUPS * MOE_PER_GROUP
LANES = 128
ROUTER_LANES = LANES
ROW_TILE = 256
MOE_TILE = 256
VMEM_LIMIT = 56 * 1024 * 1024


DMA_ISSUE_UNROLL = 8


def _cparams(*sem, unchecked_dma=False):
    return pltpu.CompilerParams(dimension_semantics=sem, vmem_limit_bytes=VMEM_LIMIT,
                                disable_bounds_checks=unchecked_dma)


def _sigmoid(x):
    return 1.0 / (1.0 + jnp.exp(-x))


def _gelu(x):
    return 0.5 * x * (1.0 + jnp.tanh(math.sqrt(2.0 / math.pi) * (x + 0.044715 * (x * x * x))))


def _split3(x):
    hi = x.astype(BF16)
    r1 = x - hi.astype(F32)
    mid = r1.astype(BF16)
    lo = (r1 - mid.astype(F32)).astype(BF16)
    return hi, mid, lo


def _mod_kernel(c_ref, w_ref, b_ref, o_ref):
    cv = c_ref[...]
    s = (cv * _sigmoid(cv)).astype(BF16)
    o_ref[...] = jnp.dot(s, w_ref[...].astype(BF16), preferred_element_type=F32) + b_ref[...]


def _modulation(cvecs, mod_w, mod_b):
    depth, d, n = mod_w.shape
    tn = 512
    return pl.pallas_call(
        _mod_kernel,
        out_shape=jax.ShapeDtypeStruct((depth, 8, n), F32),
        grid=(depth, n // tn),
        in_specs=[pl.BlockSpec((8, d), lambda l, j: (0, 0)),
                  pl.BlockSpec((None, d, tn), lambda l, j: (l, 0, j)),
                  pl.BlockSpec((None, 1, tn), lambda l, j: (l, 0, j))],
        out_specs=pl.BlockSpec((None, 8, tn), lambda l, j: (l, 0, j)),
        compiler_params=_cparams("parallel", "parallel"),
        name="modulation",
    )(cvecs, mod_w, mod_b.reshape(depth, 1, n))


def _modulated_norm(x, g_ref, sh_ref, sc_ref):
    ms = jnp.mean(x * x, axis=-1, keepdims=True)
    return x * lax.rsqrt(ms + EPS) * g_ref[...] * (1.0 + sc_ref[...]) + sh_ref[...]


def _norm_kernel(x_ref, g_ref, sh_ref, sc_ref, o_ref):
    o_ref[...] = _modulated_norm(x_ref[...], g_ref, sh_ref, sc_ref).astype(BF16)


def _norm_modulate(xt, g, shifts, scales):
    t, d = xt.shape
    tm = ROW_TILE
    sel = lambda i: (jnp.minimum(i, 1), 0, 0)
    return pl.pallas_call(
        _norm_kernel,
        out_shape=jax.ShapeDtypeStruct((t, d), BF16),
        grid=(t // tm,),
        in_specs=[pl.BlockSpec((tm, d), lambda i: (i, 0)),
                  pl.BlockSpec((1, d), lambda i: (0, 0)),
                  pl.BlockSpec((None, 1, d), sel),
                  pl.BlockSpec((None, 1, d), sel)],
        out_specs=pl.BlockSpec((tm, d), lambda i: (i, 0)),
        compiler_params=_cparams("parallel"),
        name="norm_modulate",
    )(xt, g.reshape(1, d), shifts, scales)


def _inproj_kernel(h_ref, w_ref, o_ref, wbf_ref):
    @pl.when(pl.program_id(1) == 0)
    def _():
        wbf_ref[...] = w_ref[...].astype(BF16)

    o_ref[...] = jnp.dot(h_ref[...], wbf_ref[...], preferred_element_type=F32)


def _in_projection(h, w):
    t, d = h.shape
    n = w.shape[1]
    tm = 3 * ROW_TILE if t % (3 * ROW_TILE) == 0 else ROW_TILE
    tn = 512
    return pl.pallas_call(
        _inproj_kernel,
        out_shape=jax.ShapeDtypeStruct((t, n), F32),
        grid=(n // tn, t // tm),
        in_specs=[pl.BlockSpec((tm, d), lambda j, i: (i, 0)),
                  pl.BlockSpec((d, tn), lambda j, i: (0, j))],
        out_specs=pl.BlockSpec((tm, tn), lambda j, i: (i, j)),
        scratch_shapes=[pltpu.VMEM((d, tn), BF16)],
        compiler_params=_cparams("arbitrary", "arbitrary"),
        name="in_projection",
    )(h, w)


def _outproj_kernel(a1_ref, a2_ref, w_ref, res_ref, gate_ref, o_ref, wbf_ref, *, k1):
    @pl.when(pl.program_id(1) == 0)
    def _():
        wbf_ref[...] = w_ref[...].astype(BF16)

    acc = jnp.dot(a1_ref[...], wbf_ref[:k1, :], preferred_element_type=F32)
    acc = acc + jnp.dot(a2_ref[...], wbf_ref[k1:, :], preferred_element_type=F32)
    o_ref[...] = res_ref[...] + gate_ref[...] * acc


def _out_projection(a1, a2, w, res, gates, res_tile_off, gate_of_tile):
    r, k1 = a1.shape
    k2 = a2.shape[1]
    d = w.shape[1]
    tm, tn = ROW_TILE, 1024
    return pl.pallas_call(
        functools.partial(_outproj_kernel, k1=k1),
        out_shape=jax.ShapeDtypeStruct((r, d), F32),
        grid=(d // tn, r // tm),
        in_specs=[pl.BlockSpec((tm, k1), lambda j, i: (i, 0)),
                  pl.BlockSpec((tm, k2), lambda j, i: (i, 0)),
                  pl.BlockSpec((k1 + k2, tn), lambda j, i: (0, j)),
                  pl.BlockSpec((tm, tn), lambda j, i: (i + res_tile_off, j)),
                  pl.BlockSpec((None, 1, tn), lambda j, i: (gate_of_tile(i), 0, j))],
        out_specs=pl.BlockSpec((tm, tn), lambda j, i: (i, j)),
        scratch_shapes=[pltpu.VMEM((k1 + k2, tn), BF16)],
        compiler_params=_cparams("arbitrary", "arbitrary"),
        name="out_projection",
    )(a1, a2, w, res, gates)


def _qkv_kernel(q_ref, k_ref, v_ref, cos_ref, sa_ref, sb_ref, qg_ref, kg_ref, qo_ref, ko_ref, vo_ref):
    cos, sa, sb = cos_ref[...], sa_ref[...], sb_ref[...]

    def head(x, g, scale):
        ms = jnp.mean(x * x, axis=-1, keepdims=True)
        xn = x * lax.rsqrt(ms + EPS) * g
        r = xn * cos + pltpu.roll(xn, HEAD_DIM - ROPE_PAIRS, 1) * sa + pltpu.roll(xn, ROPE_PAIRS, 1) * sb
        return (r * scale).astype(BF16)

    for h in range(Q_HEADS):
        sl = slice(h * HEAD_DIM, (h + 1) * HEAD_DIM)
        qo_ref[:, sl] = head(q_ref[:, sl], qg_ref[...], HEAD_DIM ** -0.5)
    for h in range(KV_HEADS):
        sl = slice(h * HEAD_DIM, (h + 1) * HEAD_DIM)
        ko_ref[:, sl] = head(k_ref[:, sl], kg_ref[...], 1.0)
    vo_ref[...] = v_ref[...].astype(BF16)


def _rope_tables(n_ctx, n_lat):
    t = jnp.arange(n_lat)
    row = (t // GRID_W).astype(F32)
    col = (t % GRID_W).astype(F32)
    inv_freq = ROPE_BASE ** (-jnp.arange(ROPE_PAIRS, dtype=F32) / ROPE_PAIRS)
    ang_r = row[:, None] * inv_freq
    ang_c = col[:, None] * inv_freq
    ang = jnp.concatenate([ang_r, ang_r, ang_c, ang_c], axis=-1)
    ang = jnp.concatenate([jnp.zeros((n_ctx, HEAD_DIM), F32), ang], axis=0)
    cos, sin = jnp.cos(ang), jnp.sin(ang)
    first = (jnp.arange(HEAD_DIM) % (2 * ROPE_PAIRS)) < ROPE_PAIRS
    return cos, jnp.where(first, -sin, 0.0), jnp.where(first, 0.0, sin)


def _qkv_prepare(proj, q_g, k_g, n_ctx):
    t = proj.shape[0]
    tm = ROW_TILE
    qw, kw = Q_HEADS * HEAD_DIM, KV_HEADS * HEAD_DIM
    cos, sa, sb = _rope_tables(n_ctx, t - n_ctx)
    tab = pl.BlockSpec((tm, HEAD_DIM), lambda i: (i, 0))
    vec = pl.BlockSpec((1, HEAD_DIM), lambda i: (0, 0))
    return pl.pallas_call(
        _qkv_kernel,
        out_shape=(jax.ShapeDtypeStruct((t, qw), BF16), jax.ShapeDtypeStruct((t, kw), BF16),
                   jax.ShapeDtypeStruct((t, kw), BF16)),
        grid=(t // tm,),
        in_specs=[pl.BlockSpec((tm, qw), lambda i: (i, 0)),
                  pl.BlockSpec((tm, kw), lambda i: (i, qw // kw)),
                  pl.BlockSpec((tm, kw), lambda i: (i, qw // kw + 1)),
                  tab, tab, tab, vec, vec],
        out_specs=(pl.BlockSpec((tm, qw), lambda i: (i, 0)), pl.BlockSpec((tm, kw), lambda i: (i, 0)),
                   pl.BlockSpec((tm, kw), lambda i: (i, 0))),
        compiler_params=_cparams("parallel"),
        name="qkv_prepare",
    )(proj, proj, proj, cos, sa, sb, q_g.reshape(1, HEAD_DIM), k_g.reshape(1, HEAD_DIM))


def _attn_kernel(sink_ref, q_ref, kp_ref, kc_ref, kn_ref, kx_ref, vp_ref, vc_ref, vn_ref, vx_ref, o_ref,
                 *, ctx_blocks, n_blocks):
    b = pl.program_id(0)
    blk = ATT_BLOCK
    rows = Q_PER_KV * blk
    n_ctx = ctx_blocks * blk
    latent = b >= ctx_blocks
    lat_f = jnp.where(latent, 1.0, 0.0)
    prev_f = jnp.where(jnp.logical_and(latent, b - 1 >= ctx_blocks), 1.0, 0.0)
    next_f = jnp.where(jnp.logical_and(latent, b + 1 <= n_blocks - 1), 1.0, 0.0)
    qi = lax.broadcasted_iota(jnp.int32, (rows, 3 * blk + n_ctx), 0) % blk
    col = lax.broadcasted_iota(jnp.int32, (rows, 3 * blk + n_ctx), 1)
    kj = col % blk
    allowed = jnp.where(col < blk, jnp.where(kj >= qi, prev_f, 0.0),
                        jnp.where(col < 2 * blk, lat_f,
                                  jnp.where(col < 3 * blk, jnp.where(kj <= qi, next_f, 0.0), 1.0))) > 0.5
    for h in range(KV_HEADS):
        hs = slice(h * HEAD_DIM, (h + 1) * HEAD_DIM)
        q4 = jnp.concatenate([q_ref[:, (Q_PER_KV * h + g) * HEAD_DIM:(Q_PER_KV * h + g + 1) * HEAD_DIM]
                              for g in range(Q_PER_KV)], axis=0)
        kk = jnp.concatenate([kp_ref[:, hs], kc_ref[:, hs], kn_ref[:, hs], kx_ref[:, hs]], axis=0)
        vv = jnp.concatenate([vp_ref[:, hs], vc_ref[:, hs], vn_ref[:, hs], vx_ref[:, hs]], axis=0)
        s = lax.dot_general(q4, kk, (((1,), (1,)), ((), ())), preferred_element_type=F32)
        s = jnp.where(allowed, s, NEG_INF)
        sk = jnp.concatenate([jnp.full((blk, 1), sink_ref[Q_PER_KV * h + g], F32) for g in range(Q_PER_KV)],
                             axis=0)
        m = jnp.maximum(jnp.max(s, axis=-1, keepdims=True), sk)
        p = jnp.exp(s - m)
        den = jnp.sum(p, axis=-1, keepdims=True) + jnp.exp(sk - m)
        o = jnp.dot(p.astype(BF16), vv, preferred_element_type=F32) / den
        for g in range(Q_PER_KV):
            o_ref[:, (Q_PER_KV * h + g) * HEAD_DIM:(Q_PER_KV * h + g + 1) * HEAD_DIM] = (
                o[g * blk:(g + 1) * blk, :].astype(BF16))


def _attention(q, k, v, sink, n_ctx):
    t = q.shape[0]
    blk = ATT_BLOCK
    nb = t // blk
    cb = n_ctx // blk
    kw = k.shape[1]
    prev = pl.BlockSpec((blk, kw), lambda b, s: (jnp.maximum(b - 1, 0), 0))
    cur = pl.BlockSpec((blk, kw), lambda b, s: (b, 0))
    nxt = pl.BlockSpec((blk, kw), lambda b, s: (jnp.minimum(b + 1, nb - 1), 0))
    ctx = pl.BlockSpec((n_ctx, kw), lambda b, s: (0, 0))
    return pl.pallas_call(
        functools.partial(_attn_kernel, ctx_blocks=cb, n_blocks=nb),
        out_shape=jax.ShapeDtypeStruct(q.shape, BF16),
        grid_spec=pltpu.PrefetchScalarGridSpec(
            num_scalar_prefetch=1, grid=(nb,),
            in_specs=[pl.BlockSpec((blk, q.shape[1]), lambda b, s: (b, 0)),
                      prev, cur, nxt, ctx, prev, cur, nxt, ctx],
            out_specs=pl.BlockSpec((blk, q.shape[1]), lambda b, s: (b, 0))),
        compiler_params=_cparams("parallel"),
        name="window_attention",
    )(sink, q, k, k, k, k, v, v, v, v)


def _pool_kernel(*refs, n_ctx, n_lat, n_tiles):
    ng = len(POOL_WINDOWS)
    cur_refs, prev_refs, next_refs = refs[:ng], refs[ng:2 * ng], refs[2 * ng:3 * ng]
    w_ref, scale_ref, o_ref, xe_ref = refs[3 * ng:]
    i = pl.program_id(0)
    tm = ROW_TILE
    hl = POOL_HALO
    gw = w_ref.shape[1]
    prev_valid = i >= 2
    next_valid = jnp.logical_and(i >= 1, i < n_tiles - 1)
    pos = lax.broadcasted_iota(jnp.int32, (tm, 1), 0) + jnp.where(i == 0, 0, (i - 1) * tm)
    seg_len = jnp.where(i == 0, n_ctx, n_lat)
    for g, win in enumerate(POOL_WINDOWS):
        cur = cur_refs[g][...]
        xe_ref[0:hl, :] = jnp.where(prev_valid, prev_refs[g][...], 0.0)
        xe_ref[hl:hl + tm, :] = cur
        xe_ref[hl + tm:hl + tm + hl, :] = jnp.where(next_valid, next_refs[g][...], 0.0)
        acc = xe_ref[hl - win // 2:hl - win // 2 + tm, :]
        for o in range(1 - win // 2, win // 2):
            acc = acc + xe_ref[hl + o:hl + o + tm, :]
        cnt = jnp.minimum(pos - win // 2 + win, seg_len) - jnp.maximum(pos - win // 2, 0)
        d = acc / cnt.astype(F32) - cur
        y = jnp.dot(d.astype(BF16), w_ref[g].astype(BF16), preferred_element_type=F32)
        o_ref[:, g * gw:(g + 1) * gw] = (y * scale_ref[:, g * gw:(g + 1) * gw]).astype(BF16)


def _pool_mixer(proj, col0, w_pool, pool_scale, n_ctx):
    t = proj.shape[0]
    tm, hl = ROW_TILE, POOL_HALO
    ng, gw = w_pool.shape[0], w_pool.shape[1]
    nt = t // tm
    cb = col0 // gw
    per = tm // hl
    cur = [pl.BlockSpec((tm, gw), functools.partial(lambda i, g: (i, cb + g), g=g)) for g in range(ng)]
    prev = [pl.BlockSpec((hl, gw), functools.partial(lambda i, g: (jnp.maximum(i * per - 1, 0), cb + g), g=g))
            for g in range(ng)]
    nxt = [pl.BlockSpec((hl, gw), functools.partial(lambda i, g: (jnp.minimum((i + 1) * per, nt * per - 1), cb + g), g=g))
           for g in range(ng)]
    return pl.pallas_call(
        functools.partial(_pool_kernel, n_ctx=n_ctx, n_lat=t - n_ctx, n_tiles=nt),
        out_shape=jax.ShapeDtypeStruct((t, ng * gw), BF16),
        grid=(nt,),
        in_specs=cur + prev + nxt + [pl.BlockSpec((ng, gw, gw), lambda i: (0, 0, 0)),
                                     pl.BlockSpec((1, ng * gw), lambda i: (0, 0))],
        out_specs=pl.BlockSpec((tm, ng * gw), lambda i: (i, 0)),
        scratch_shapes=[pltpu.VMEM((tm + 2 * hl, gw), F32)],
        compiler_params=_cparams("parallel"),
        name="pool_mixer",
    )(*([proj] * (3 * ng)), w_pool, pool_scale.reshape(1, ng * gw))


def _s5_powers(lr, li, dt, kcol):
    mag = jnp.exp(kcol * (lr * dt))
    ang = kcol * (li * dt)
    return mag * jnp.cos(ang), mag * jnp.sin(ang)


def _s5_expand(z_re, z_im, e_re, e_im):
    n, p = e_re.shape
    hh = z_re.shape[0]
    er = jnp.broadcast_to(e_re[:, None, :], (n, hh, p)).reshape(n * hh, p)
    ei = jnp.broadcast_to(e_im[:, None, :], (n, hh, p)).reshape(n * hh, p)
    zr = jnp.broadcast_to(z_re[None, :, :], (n, hh, p)).reshape(n * hh, p)
    zi = jnp.broadcast_to(z_im[None, :, :], (n, hh, p)).reshape(n * hh, p)
    return zr * er - zi * ei, zr * ei + zi * er


def _s5_discretize_t(lr, li, dt, bt_re, bt_im):
    mag = jnp.exp(lr * dt)
    a_re, a_im = mag * jnp.cos(li * dt), mag * jnp.sin(li * dt)
    den = lr * lr + li * li
    n_re = a_re - 1.0
    k_re = (n_re * lr + a_im * li) / den
    k_im = (a_im * lr - n_re * li) / den
    return k_re * bt_re - k_im * bt_im, k_re * bt_im + k_im * bt_re


def _s5_group_params(lam_re_ref, lam_im_ref, ldt_ref, bt_re_ref, bt_im_ref, d):
    lr, li = lam_re_ref[d], lam_im_ref[d]
    dt = jnp.exp(ldt_ref[d])
    bb_re, bb_im = _s5_discretize_t(lr, li, dt, bt_re_ref[d], bt_im_ref[d])
    return lr, li, dt, bb_re, bb_im


def _s5_state_kernel(u_ref, lam_re_ref, lam_im_ref, ldt_ref, bt_re_ref, bt_im_ref, sf_ref, sb_ref):
    lc = S5_CHUNK
    u = u_ref[...].astype(BF16)
    t_col = lax.broadcasted_iota(jnp.int32, (lc, 1), 0).astype(F32)
    for d, out_ref, kcol in ((0, sf_ref, (lc - 1) - t_col), (1, sb_ref, t_col)):
        lr, li, dt, bb_re, bb_im = _s5_group_params(lam_re_ref, lam_im_ref, ldt_ref, bt_re_ref, bt_im_ref, d)
        e_re, e_im = _s5_powers(lr, li, dt, kcol)
        w_re, w_im = _s5_expand(bb_re, bb_im, e_re, e_im)
        w = jnp.concatenate([w_re, w_im], axis=1).astype(BF16)
        out_ref[...] = jnp.dot(u, w, preferred_element_type=F32)


def _s5_carry_kernel(sf_ref, sb_ref, lam_re_ref, lam_im_ref, ldt_ref, hf_ref, hb_ref, sw_ref, *, ctx_chunks):
    nc = sf_ref.shape[0]
    lanes = sf_ref.shape[1]
    first = (lax.broadcasted_iota(jnp.int32, (1, lanes), 1) % (2 * S5_P)) < S5_P

    def scan(d, s_ref, h_ref, order):
        lr, li, dt = lam_re_ref[d], lam_im_ref[d], jnp.exp(ldt_ref[d])
        mag = jnp.exp(S5_CHUNK * (lr * dt))
        ang = S5_CHUNK * (li * dt)
        pa = mag * jnp.cos(ang)
        a_im = mag * jnp.sin(ang)
        qa = jnp.where(first, -a_im, a_im)
        s_all = s_ref[...]
        sw_ref[...] = jnp.where(first, pltpu.roll(s_all, lanes - S5_P, 1), pltpu.roll(s_all, S5_P, 1))
        zero = jnp.zeros((1, lanes), F32)

        def run(lo, n, step, carry):
            def body(k, hc):
                h, hs = hc
                c = lo + k * step
                h_ref[pl.ds(c, 1), :] = h
                s = s_ref[pl.ds(c, 1), :]
                ss = sw_ref[pl.ds(c, 1), :]
                return pa * h + qa * hs + s, pa * hs - qa * h + ss
            return lax.fori_loop(0, n, body, carry)

        carry = (zero, zero)
        for lo, n, step in order:
            carry = run(lo, n, step, carry)

    scan(0, sf_ref, hf_ref, ((0, nc, 1),))
    scan(1, sb_ref, hb_ref, ((ctx_chunks - 1, ctx_chunks, -1), (nc - 1, nc - ctx_chunks, -1)))


def _s5_output_kernel(u_ref, hf_ref, hb_ref, lam_re_ref, lam_im_ref, ldt_ref, bt_re_ref, bt_im_ref,
                      c_re_ref, c_im_ref, y_ref):
    lc = S5_CHUNK
    half = lc // 2
    n = lc * S5_H
    u = u_ref[...].astype(BF16)
    t_col = lax.broadcasted_iota(jnp.int32, (lc, 1), 0).astype(F32)
    s_idx = lax.broadcasted_iota(jnp.int32, (n, n), 0) // S5_H
    t_idx = lax.broadcasted_iota(jnp.int32, (n, n), 1) // S5_H
    mt = jnp.zeros((n, n), F32)
    y = jnp.zeros((u.shape[0], n), F32)
    for d, h_ref in ((0, hf_ref), (1, hb_ref)):
        lr, li, dt, bb_re, bb_im = _s5_group_params(lam_re_ref, lam_im_ref, ldt_ref, bt_re_ref, bt_im_ref, d)
        c_re, c_im = c_re_ref[d], c_im_ref[d]
        if d == 0:
            kl, kr, kc = t_col - half, half - t_col, t_col + 1.0
            keep = s_idx <= t_idx
        else:
            kl, kr, kc = half - t_col, t_col - half, lc - t_col
            keep = s_idx >= t_idx
        l_re, l_im = _s5_expand(c_re, c_im, *_s5_powers(lr, li, dt, kl))
        r_re, r_im = _s5_expand(bb_re, bb_im, *_s5_powers(lr, li, dt, kr))
        lst = jnp.concatenate([l_re, -l_im], axis=1)
        rst = jnp.concatenate([r_re, r_im], axis=1)
        r3, l3 = _split3(rst), _split3(lst)
        m = jnp.zeros((n, n), F32)
        for ra, la in ((0, 0), (0, 1), (1, 0), (0, 2), (1, 1), (2, 0)):
            m = m + lax.dot_general(r3[ra], l3[la], (((1,), (1,)), ((), ())), preferred_element_type=F32)
        mt = mt + jnp.where(keep, m, 0.0)
        ca_re, ca_im = _s5_expand(c_re, c_im, *_s5_powers(lr, li, dt, kc))
        t2 = jnp.concatenate([ca_re, -ca_im], axis=1).astype(BF16)
        y = y + lax.dot_general(h_ref[...].astype(BF16), t2, (((1,), (1,)), ((), ())),
                                preferred_element_type=F32)
    y_ref[...] = y + jnp.dot(u, mt.astype(BF16), preferred_element_type=F32)


def _s5_scan(s_all, n_ctx, lam_re, lam_im, log_dt, b_re, b_im, c_re, c_im):
    t, cw = s_all.shape
    g, p, h, lc = cw // S5_H, S5_P, S5_H, S5_CHUNK
    nc = t // lc
    ug = s_all.reshape(nc, lc, g, h).transpose(2, 0, 1, 3).reshape(g, nc, lc * h)
    lam_re4 = lam_re.transpose(1, 0, 2)[:, :, None, :]
    lam_im4 = lam_im.transpose(1, 0, 2)[:, :, None, :]
    ldt4 = jnp.broadcast_to(log_dt.T[:, :, None, None], (g, 2, 1, p))
    bt_re = b_re.transpose(1, 0, 3, 2)
    bt_im = b_im.transpose(1, 0, 3, 2)
    c_re4 = c_re.transpose(1, 0, 2, 3)
    c_im4 = c_im.transpose(1, 0, 2, 3)
    vec4 = pl.BlockSpec((None, 2, 1, p), lambda i: (i, 0, 0, 0))
    mat4 = pl.BlockSpec((None, 2, h, p), lambda i: (i, 0, 0, 0))
    u_spec = pl.BlockSpec((None, nc, lc * h), lambda i: (i, 0, 0))
    st_spec = pl.BlockSpec((nc, 2 * p), lambda i: (0, i))
    st_shape = jax.ShapeDtypeStruct((nc, g * 2 * p), F32)
    s_f, s_b = pl.pallas_call(
        _s5_state_kernel, out_shape=(st_shape, st_shape), grid=(g,),
        in_specs=[u_spec, vec4, vec4, vec4, mat4, mat4],
        out_specs=(st_spec, st_spec),
        compiler_params=_cparams("parallel"), name="s5_chunk_states",
    )(ug, lam_re4, lam_im4, ldt4, bt_re, bt_im)

    def lanes(x):
        return jnp.concatenate([x, x], axis=-1).reshape(2, 1, g * 2 * p)
    cols = 8 * 2 * p
    row = pl.BlockSpec((2, 1, cols), lambda j: (0, 0, j))
    blk = pl.BlockSpec((nc, cols), lambda j: (0, j))
    h_f, h_b = pl.pallas_call(
        functools.partial(_s5_carry_kernel, ctx_chunks=n_ctx // lc),
        out_shape=(st_shape, st_shape), grid=(g * 2 * p // cols,),
        in_specs=[blk, blk, row, row, row], out_specs=(blk, blk),
        scratch_shapes=[pltpu.VMEM((nc, cols), F32)],
        compiler_params=_cparams("parallel"), name="s5_chunk_carry",
    )(s_f, s_b, lanes(lam_re), lanes(lam_im), lanes(jnp.broadcast_to(log_dt[:, :, None], (2, g, p))))

    yg = pl.pallas_call(
        _s5_output_kernel, out_shape=jax.ShapeDtypeStruct((g, nc, lc * h), F32), grid=(g,),
        in_specs=[u_spec, st_spec, st_spec, vec4, vec4, vec4, mat4, mat4, mat4, mat4],
        out_specs=u_spec,
        compiler_params=_cparams("parallel"), name="s5_outputs",
    )(ug, h_f, h_b, lam_re4, lam_im4, ldt4, bt_re, bt_im, c_re4, c_im4)
    return yg.reshape(g, nc, lc, h).transpose(1, 2, 0, 3).reshape(t, cw)


def _glu_kernel(y_ref, u_ref, dk_ref, w_ref, b_ref, o_ref, wbf_ref):
    @pl.when(pl.program_id(0) == 0)
    def _():
        wbf_ref[...] = w_ref[...].astype(BF16)

    yl = dk_ref[...] * u_ref[...] + y_ref[...]
    gl = _gelu(yl)
    z = jnp.dot(gl.astype(BF16), wbf_ref[...], preferred_element_type=F32) + b_ref[...]
    o_ref[...] = (gl * _sigmoid(z)).astype(BF16)


def _s5_glu(y_scan, proj, d_skip, glu_w, glu_b, n_ctx):
    t, cw = y_scan.shape
    tm = ROW_TILE
    off = n_ctx // tm
    n_lat = t - n_ctx
    vec = pl.BlockSpec((1, cw), lambda i: (0, 0))
    return pl.pallas_call(
        _glu_kernel, out_shape=jax.ShapeDtypeStruct((n_lat, cw), BF16), grid=(n_lat // tm,),
        in_specs=[pl.BlockSpec((tm, cw), lambda i: (i + off, 0)),
                  pl.BlockSpec((tm, cw), lambda i: (i + off, 0)),
                  vec, pl.BlockSpec((cw, cw), lambda i: (0, 0)), vec],
        out_specs=pl.BlockSpec((tm, cw), lambda i: (i, 0)),
        scratch_shapes=[pltpu.VMEM((cw, cw), BF16)],
        compiler_params=_cparams("arbitrary"), name="s5_glu",
    )(y_scan, proj, d_skip.reshape(1, cw), glu_w, glu_b.reshape(1, cw))


def _gmlp_kernel(*refs, n_col):
    u_refs, v_refs = refs[:n_col], refs[n_col:2 * n_col]
    lng_ref, lnb_ref, ws_ref, bst_ref, o_ref = refs[2 * n_col:]
    u = _gelu(jnp.concatenate([r[...] for r in u_refs], axis=1))
    v = _gelu(jnp.concatenate([r[...] for r in v_refs], axis=1))
    mu = jnp.mean(v, axis=-1, keepdims=True)
    vc = v - mu
    var = jnp.mean(vc * vc, axis=-1, keepdims=True)
    vn = (vc * lax.rsqrt(var + EPS) * lng_ref[...] + lnb_ref[...]).astype(BF16)
    gw = u.shape[1] // SG_HEADS
    for c in range(u.shape[0] // SG_CHUNK):
        rs = slice(c * SG_CHUNK, (c + 1) * SG_CHUNK)
        for g in range(SG_HEADS):
            cs = slice(g * gw, (g + 1) * gw)
            mixed = jnp.dot(ws_ref[g].astype(BF16), vn[rs, cs], preferred_element_type=F32)
            mixed = mixed + bst_ref[:, g:g + 1]
            o_ref[rs, cs] = (u[rs, cs] * mixed).astype(BF16)


def _gmlp_mixer(proj, col_u, col_v, width, ln_g, ln_b, w_s, b_s, n_ctx):
    t = proj.shape[0]
    tm, cwid = ROW_TILE, 1024
    off = n_ctx // tm
    n_lat = t - n_ctx
    n_col = width // cwid
    specs = [pl.BlockSpec((tm, cwid), functools.partial(lambda i, cb: (i + off, cb), cb=(c0 // cwid) + k))
             for c0 in (col_u, col_v) for k in range(n_col)]
    vec = pl.BlockSpec((1, width), lambda i: (0, 0))
    return pl.pallas_call(
        functools.partial(_gmlp_kernel, n_col=n_col),
        out_shape=jax.ShapeDtypeStruct((n_lat, width), BF16), grid=(n_lat // tm,),
        in_specs=specs + [vec, vec, pl.BlockSpec(w_s.shape, lambda i: (0, 0, 0)),
                          pl.BlockSpec((SG_CHUNK, SG_HEADS), lambda i: (0, 0))],
        out_specs=pl.BlockSpec((tm, width), lambda i: (i, 0)),
        compiler_params=_cparams("parallel"), name="gmlp_mixer",
    )(*([proj] * (2 * n_col)), ln_g.reshape(1, width), ln_b.reshape(1, width), w_s, b_s.T)


def _router_kernel(x_ref, g_ref, sh_ref, sc_ref, wr_ref, br_ref, hp_ref, rt_ref, cnt_ref, w3_ref, run_ref):
    @pl.when(pl.program_id(0) == 0)
    def _():
        hi, mid, lo = _split3(wr_ref[...])
        w3_ref[0], w3_ref[1], w3_ref[2] = hi, mid, lo
        run_ref[...] = jnp.zeros_like(run_ref)

    h = _modulated_norm(x_ref[...], g_ref, sh_ref, sc_ref)
    tm = h.shape[0]
    half = h.shape[1] // 2
    packed = pltpu.pack_elementwise([h[:, :half], h[:, half:]], packed_dtype=jnp.bfloat16)
    slab = half // LANES
    for s in range(slab):
        hp_ref[pl.ds(s, tm, stride=slab), :] = packed[:, s * LANES:(s + 1) * LANES]

    h3 = _split3(h)
    logits = br_ref[...]
    for a, b in ((0, 0), (0, 1), (1, 0), (0, 2), (1, 1), (2, 0)):
        logits = logits + jnp.dot(h3[a], w3_ref[b], preferred_element_type=F32)

    lane = lax.broadcasted_iota(jnp.int32, logits.shape, 1)
    big = jnp.int32(ROUTER_LANES)
    gl = jnp.where(lane < MOE_GROUPS, logits, NEG_INF)
    gm = jnp.max(gl, axis=-1, keepdims=True)
    g_w = 1.0 / jnp.sum(jnp.exp(gl - gm), axis=-1, keepdims=True)
    g_idx = jnp.min(jnp.where(gl == gm, lane, big), axis=-1, keepdims=True)
    lo_lane = MOE_GROUPS + g_idx * MOE_PER_GROUP
    fl = jnp.where(jnp.logical_and(lane >= lo_lane, lane < lo_lane + MOE_PER_GROUP), logits, NEG_INF)
    m1 = jnp.max(fl, axis=-1, keepdims=True)
    i1 = jnp.min(jnp.where(fl == m1, lane, big), axis=-1, keepdims=True)
    fl2 = jnp.where(lane == i1, NEG_INF, fl)
    m2 = jnp.max(fl2, axis=-1, keepdims=True)
    i2 = jnp.min(jnp.where(fl2 == m2, lane, big), axis=-1, keepdims=True)
    e2 = jnp.exp(m2 - m1)
    w1 = g_w / (1.0 + e2)
    w2 = g_w * e2 / (1.0 + e2)
    oh1 = jnp.where(lane == i1 - MOE_GROUPS, 1.0, 0.0)
    oh2 = jnp.where(lane == i2 - MOE_GROUPS, 1.0, 0.0)
    both = oh1 + oh2
    tri = jnp.where(lax.broadcasted_iota(jnp.int32, (tm, tm), 1) < lax.broadcasted_iota(jnp.int32, (tm, tm), 0),
                    1.0, 0.0).astype(BF16)
    before = jnp.dot(tri, both.astype(BF16), preferred_element_type=F32) + run_ref[...]
    r1 = jnp.sum(before * oh1, axis=-1, keepdims=True)
    r2 = jnp.sum(before * oh2, axis=-1, keepdims=True)
    run_ref[...] = run_ref[...] + jnp.sum(both, axis=0, keepdims=True)
    cnt_ref[...] = jnp.broadcast_to(run_ref[...], cnt_ref.shape)
    rt_ref[...] = jnp.where(lane == 0, (i1 - MOE_GROUPS).astype(F32),
                            jnp.where(lane == 1, (i2 - MOE_GROUPS).astype(F32),
                                      jnp.where(lane == 2, w1,
                                                jnp.where(lane == 3, w2,
                                                          jnp.where(lane == 4, r1, jnp.where(lane == 5, r2, 0.0))))))


def _moe_router(xt, tile_off, n_rows, g, shifts, scales, first_latent_tile, w_grp, b_grp, w_rt, b_rt):
    d = xt.shape[1]
    tm = ROW_TILE
    slab = d // 2 // LANES
    ne = MOE_GROUPS * MOE_PER_GROUP
    wr = jnp.concatenate([w_grp, w_rt.transpose(1, 0, 2).reshape(d, ne),
                          jnp.zeros((d, ROUTER_LANES - MOE_GROUPS - ne), F32)], axis=1)
    br = jnp.concatenate([b_grp, b_rt.reshape(ne), jnp.zeros((ROUTER_LANES - MOE_GROUPS - ne,), F32)])
    sel = lambda i: (jnp.where(i + tile_off >= first_latent_tile, 1, 0), 0, 0)
    return pl.pallas_call(
        _router_kernel,
        out_shape=(jax.ShapeDtypeStruct((n_rows * slab, LANES), jnp.uint32),
                   jax.ShapeDtypeStruct((n_rows, ROUTER_LANES), F32),
                   jax.ShapeDtypeStruct((8, ROUTER_LANES), F32)),
        grid=(n_rows // tm,),
        in_specs=[pl.BlockSpec((tm, d), lambda i: (i + tile_off, 0)),
                  pl.BlockSpec((1, d), lambda i: (0, 0)),
                  pl.BlockSpec((None, 1, d), sel), pl.BlockSpec((None, 1, d), sel),
                  pl.BlockSpec((d, ROUTER_LANES), lambda i: (0, 0)),
                  pl.BlockSpec((1, ROUTER_LANES), lambda i: (0, 0))],
        out_specs=(pl.BlockSpec((tm * slab, LANES), lambda i: (i, 0)),
                   pl.BlockSpec((tm, ROUTER_LANES), lambda i: (i, 0)),
                   pl.BlockSpec((8, ROUTER_LANES), lambda i: (0, 0))),
        scratch_shapes=[pltpu.VMEM((3, d, ROUTER_LANES), BF16), pltpu.VMEM((1, ROUTER_LANES), F32)],
        compiler_params=_cparams("arbitrary"), name="moe_router",
    )(xt, g.reshape(1, d), shifts, scales, wr, br.reshape(1, ROUTER_LANES))


def _row_gather_kernel(nu_ref, idx_ref, src_hbm, out_ref, sem, *, slab):
    t = pl.program_id(0)
    tm = out_ref.shape[0] // slab

    @pl.when(t < nu_ref[0])
    def _():
        def issue(r, carry):
            src = pl.multiple_of(idx_ref[t * tm + r] * slab, slab)
            dst = pl.multiple_of(r * slab, slab)
            pltpu.make_async_copy(src_hbm.at[pl.ds(src, slab)], out_ref.at[pl.ds(dst, slab)], sem).start()
            return carry

        lax.fori_loop(0, tm, issue, 0, unroll=DMA_ISSUE_UNROLL)
        pltpu.make_async_copy(src_hbm.at[pl.ds(0, tm * slab)], out_ref, sem).wait()


def _moe_dispatch(hp, slab, src_tok, n_used, n_tiles):
    tm = MOE_TILE
    last = lambda t, nu, idx: (jnp.minimum(t, nu[0] - 1), 0)
    return pl.pallas_call(
        functools.partial(_row_gather_kernel, slab=slab),
        out_shape=jax.ShapeDtypeStruct((n_tiles * tm * slab, LANES), hp.dtype),
        grid_spec=pltpu.PrefetchScalarGridSpec(
            num_scalar_prefetch=2, grid=(n_tiles,),
            in_specs=[pl.BlockSpec(memory_space=pl.ANY)],
            out_specs=pl.BlockSpec((tm * slab, LANES), last),
            scratch_shapes=[pltpu.SemaphoreType.DMA(())]),
        compiler_params=_cparams("arbitrary", unchecked_dma=True), name="moe_dispatch",
    )(n_used, src_tok, hp)


def _unpack_pairs(xp):
    lo = pltpu.unpack_elementwise(xp, index=0, packed_dtype=jnp.bfloat16, unpacked_dtype=F32)
    hi = pltpu.unpack_elementwise(xp, index=1, packed_dtype=jnp.bfloat16, unpacked_dtype=F32)
    return lo.astype(BF16), hi.astype(BF16)


def _expert_up_kernel(te_ref, nu_ref, xs_ref, wg_ref, wu_ref, h_ref, wgb_ref, wub_ref):
    t = pl.program_id(0)
    tc = jnp.minimum(t, nu_ref[0] - 1)
    new_expert = jnp.logical_or(t == 0, te_ref[tc] != te_ref[jnp.maximum(tc - 1, 0)])

    @pl.when(t < nu_ref[0])
    def _():
        @pl.when(new_expert)
        def _():
            wgb_ref[...] = wg_ref[...].astype(BF16)
            wub_ref[...] = wu_ref[...].astype(BF16)

        tm = h_ref.shape[0]
        slab = xs_ref.shape[0] // tm
        parts = [_unpack_pairs(xs_ref[pl.ds(s, tm, stride=slab), :]) for s in range(slab)]
        lo = jnp.concatenate([p[0] for p in parts], axis=1)
        hi = jnp.concatenate([p[1] for p in parts], axis=1)
        half = lo.shape[1]
        gate = (jnp.dot(lo, wgb_ref[:half, :], preferred_element_type=F32)
                + jnp.dot(hi, wgb_ref[half:, :], preferred_element_type=F32))
        up = (jnp.dot(lo, wub_ref[:half, :], preferred_element_type=F32)
              + jnp.dot(hi, wub_ref[half:, :], preferred_element_type=F32))
        h_ref[...] = (gate * _sigmoid(gate) * up).astype(BF16)


def _expert_down_kernel(te_ref, nu_ref, h_ref, wd_ref, y_ref, wdb_ref):
    t = pl.program_id(0)
    tc = jnp.minimum(t, nu_ref[0] - 1)
    new_expert = jnp.logical_or(t == 0, te_ref[tc] != te_ref[jnp.maximum(tc - 1, 0)])

    @pl.when(t < nu_ref[0])
    def _():
        @pl.when(new_expert)
        def _():
            wdb_ref[...] = wd_ref[...].astype(BF16)

        y_ref[...] = jnp.dot(h_ref[...], wdb_ref[...], preferred_element_type=F32)


def _moe_experts(xs, tile_expert, n_used, layer, w_gate, w_up, w_down):
    tm = MOE_TILE
    d, ff = w_gate.shape[2], w_gate.shape[3]
    in_slab = d // 2 // LANES
    npad = xs.shape[0] // in_slab
    nt = npad // tm
    tile = lambda t, nu: jnp.minimum(t, nu[0] - 1)
    hmid = pl.pallas_call(
        _expert_up_kernel,
        out_shape=jax.ShapeDtypeStruct((npad, ff), BF16),
        grid_spec=pltpu.PrefetchScalarGridSpec(
            num_scalar_prefetch=2, grid=(nt,),
            in_specs=[pl.BlockSpec((tm * in_slab, LANES), lambda t, te, nu: (tile(t, nu), 0)),
                      pl.BlockSpec((None, None, d, ff), lambda t, te, nu: (layer, te[tile(t, nu)], 0, 0)),
                      pl.BlockSpec((None, None, d, ff), lambda t, te, nu: (layer, te[tile(t, nu)], 0, 0))],
            out_specs=pl.BlockSpec((tm, ff), lambda t, te, nu: (tile(t, nu), 0)),
            scratch_shapes=[pltpu.VMEM((d, ff), BF16), pltpu.VMEM((d, ff), BF16)]),
        compiler_params=_cparams("arbitrary"), name="moe_expert_up",
    )(tile_expert, n_used, xs, w_gate, w_up)
    return pl.pallas_call(
        _expert_down_kernel,
        out_shape=jax.ShapeDtypeStruct((npad, d), F32),
        grid_spec=pltpu.PrefetchScalarGridSpec(
            num_scalar_prefetch=2, grid=(nt,),
            in_specs=[pl.BlockSpec((tm, ff), lambda t, te, nu: (tile(t, nu), 0)),
                      pl.BlockSpec((None, None, ff, d), lambda t, te, nu: (layer, te[tile(t, nu)], 0, 0))],
            out_specs=pl.BlockSpec((tm, d), lambda t, te, nu: (tile(t, nu), 0)),
            scratch_shapes=[pltpu.VMEM((ff, d), BF16)]),
        compiler_params=_cparams("arbitrary"), name="moe_expert_down",
    )(tile_expert, n_used, hmid, w_down)


def _combine_kernel(pos_ref, y_hbm, rt_ref, res_ref, gate_ref, o_ref, buf_ref, sem):
    i = pl.program_id(0)
    tm = o_ref.shape[0]

    def issue(r, carry):
        for k in range(2):
            p = pos_ref[2 * (i * tm + r) + k]
            pltpu.make_async_copy(y_hbm.at[pl.ds(p, 1)], buf_ref.at[k, pl.ds(r, 1)], sem).start()
        return carry

    lax.fori_loop(0, tm, issue, 0, unroll=DMA_ISSUE_UNROLL)
    pltpu.make_async_copy(y_hbm.at[pl.ds(0, 2 * tm)], buf_ref.reshape(2 * tm, buf_ref.shape[2]), sem).wait()
    mix = rt_ref[:, 2:3] * buf_ref[0] + rt_ref[:, 3:4] * buf_ref[1]
    o_ref[...] = res_ref[...] + gate_ref[...] * mix


def _moe_combine(y_sorted, pos, route, res, res_tile_off, n_rows, gates, first_latent_tile):
    d = res.shape[1]
    tm = ROW_TILE
    sel = lambda i, p: (jnp.where(i + res_tile_off >= first_latent_tile, 1, 0), 0, 0)
    return pl.pallas_call(
        _combine_kernel,
        out_shape=jax.ShapeDtypeStruct((n_rows, d), F32),
        grid_spec=pltpu.PrefetchScalarGridSpec(
            num_scalar_prefetch=1, grid=(n_rows // tm,),
            in_specs=[pl.BlockSpec(memory_space=pl.ANY),
                      pl.BlockSpec((tm, ROUTER_LANES), lambda i, p: (i, 0)),
                      pl.BlockSpec((tm, d), lambda i, p: (i + res_tile_off, 0)),
                      pl.BlockSpec((None, 1, d), sel)],
            out_specs=pl.BlockSpec((tm, d), lambda i, p: (i, 0)),
            scratch_shapes=[pltpu.VMEM((2, tm, d), F32), pltpu.SemaphoreType.DMA(())]),
        compiler_params=_cparams("arbitrary", unchecked_dma=True), name="moe_combine",
    )(pos.reshape(-1), y_sorted, route, res, gates)


def _routing_plan(route, counts_row, n_tiles):
    tm = MOE_TILE
    n = route.shape[0]
    ids = route[:, 0:2].astype(jnp.int32)
    rank = route[:, 4:6].astype(jnp.int32)
    counts = counts_row[:MOE_EXPERTS].astype(jnp.int32)
    tiles = (counts + tm - 1) // tm
    tile_end = jnp.cumsum(tiles)
    tile_start = tile_end - tiles
    onehot = ids[:, :, None] == jnp.arange(MOE_EXPERTS)[None, None, :]
    pos = jnp.sum(jnp.where(onehot, tile_start[None, None, :] * tm, 0), axis=-1) + rank
    pos = pos.reshape(-1)
    tok = jnp.arange(2 * n, dtype=jnp.int32) // 2
    src_tok = jnp.zeros((n_tiles * tm,), jnp.int32).at[pos].set(tok)
    n_used = tile_end[-1].astype(jnp.int32).reshape(1)
    tile_ids = jnp.arange(n_tiles, dtype=jnp.int32)
    tile_expert = jnp.minimum(jnp.sum((tile_end[None, :] <= tile_ids[:, None]).astype(jnp.int32), axis=1),
                              MOE_EXPERTS - 1)
    return pos.astype(jnp.int32), src_tok, tile_expert, n_used


def _hier_moe(xt, tile_off, n_rows, norm_g, shifts, scales, gates, first_latent_tile, layer, moe):
    w_grp, b_grp, w_rt, b_rt, w_gate, w_up, w_down = moe
    hp, route, counts = _moe_router(xt, tile_off, n_rows, norm_g, shifts, scales, first_latent_tile,
                                    w_grp, b_grp, w_rt, b_rt)
    n_tiles = (2 * n_rows) // MOE_TILE + MOE_EXPERTS
    pos, src_tok, tile_expert, n_used = _routing_plan(route, counts[0], n_tiles)
    xs = _moe_dispatch(hp, xt.shape[1] // 2 // LANES, src_tok, n_used, n_tiles)
    y_sorted = _moe_experts(xs, tile_expert, n_used, layer, w_gate, w_up, w_down)
    return _moe_combine(y_sorted, pos, route, xt, tile_off, n_rows, gates, first_latent_tile)


def kernel(x, c, ctx, c_ctx, mod_w, mod_b, norm1_g, norm2_g, ab_w_in, ab_q_norm, ab_k_norm, ab_sink, ab_pool_w, ab_pool_scale, ab_w_out, cd_w_in, s5_lam_re, s5_lam_im, s5_log_dt, s5_b_re, s5_b_im, s5_c_re, s5_c_im, s5_d, s5_glu_w, s5_glu_b, sg_ln_g, sg_ln_b, sg_w, sg_b, cd_w_out, moe_w_group, moe_b_group, moe_w_router, moe_b_router, moe_w_gate, moe_w_up, moe_w_down):
    batch, n_lat, d = x.shape
    n_ctx = ctx.shape[1]
    assert batch == 1 and n_ctx == ROW_TILE and n_lat % ROW_TILE == 0 and mod_w.shape[0] == 2
    ctx_tiles = n_ctx // ROW_TILE

    cvecs = jnp.concatenate([c_ctx[None, :], c, jnp.zeros((6, d), F32)], axis=0)
    mod = _modulation(cvecs, mod_w, mod_b)[:, 0:2, :].reshape(2, 2, 6, 1, d)
    xt = jnp.concatenate([ctx[0], x[0]], axis=0)

    def moe_params(layer):
        return (moe_w_group[layer], moe_b_group[layer], moe_w_router[layer], moe_b_router[layer],
                moe_w_gate, moe_w_up, moe_w_down)

    m0 = mod[0]
    sh1, sc1, g1, sh2, sc2, g2 = (m0[:, k] for k in range(6))
    h = _norm_modulate(xt, norm1_g[0], sh1, sc1)
    proj = _in_projection(h, ab_w_in[0])
    q, k, v = _qkv_prepare(proj, ab_q_norm[0], ab_k_norm[0], n_ctx)
    att = _attention(q, k, v, ab_sink[0], n_ctx)
    pooled = _pool_mixer(proj, Q_HEADS * HEAD_DIM + 2 * KV_HEADS * HEAD_DIM, ab_pool_w[0], ab_pool_scale[0], n_ctx)
    xt = _out_projection(att, pooled, ab_w_out[0], xt, g1, 0, lambda i: jnp.minimum(i, 1))
    xt = _hier_moe(xt, 0, n_ctx + n_lat, norm2_g[0], sh2, sc2, g2, ctx_tiles, 0, moe_params(0))

    m1 = mod[1]
    sh1, sc1, g1, sh2, sc2, g2 = (m1[:, k] for k in range(6))
    h = _norm_modulate(xt, norm1_g[1], sh1, sc1)
    proj = _in_projection(h, cd_w_in[0])
    cw = s5_d.shape[1]
    dw = sg_ln_g.shape[1]
    y_scan = _s5_scan(proj[:, :cw], n_ctx, s5_lam_re[0], s5_lam_im[0], s5_log_dt[0], s5_b_re[0], s5_b_im[0],
                      s5_c_re[0], s5_c_im[0])
    y_c = _s5_glu(y_scan, proj, s5_d[0], s5_glu_w[0], s5_glu_b[0], n_ctx)
    y_d = _gmlp_mixer(proj, cw, cw + dw, dw, sg_ln_g[0], sg_ln_b[0], sg_w[0], sg_b[0], n_ctx)
    xl = _out_projection(y_c, y_d, cd_w_out[0], xt, g1, ctx_tiles, lambda i: 1)
    out = _hier_moe(xl, 0, n_lat, norm2_g[1], sh2, sc2, g2, 0, 1, moe_params(1))
    return out[None]
```

```python
import functools
import math

import jax
import jax.numpy as jnp
from jax import lax
from jax.experimental import pallas as pl
from jax.experimental.pallas import tpu as pltpu

F32 = jnp.float32
BF16 = jnp.bfloat16

EPS = 1e-6
NEG_INF = -1e30
GRID_W = 64
HEAD_DIM = 128
Q_HEADS = 16
KV_HEADS = 4
Q_PER_KV = Q_HEADS // KV_HEADS
ATT_BLOCK = 128
ROPE_BASE = 10000.0
ROPE_PAIRS = HEAD_DIM // 4
POOL_WINDOWS = (2, 4, 8, 16)
POOL_HALO = 8
S5_H = 16
S5_P = 64
S5_CHUNK = 32
SG_HEADS = 8
SG_CHUNK = 128
MOE_GROUPS = 4
MOE_PER_GROUP = 8
MOE_EXPERTS = MOE_GROUPS * MOE_PER_GROUP
LANES = 128
ROUTER_LANES = LANES
ROW_TILE = 256
MOE_TILE = 256
VMEM_LIMIT = 56 * 1024 * 1024


DMA_ISSUE_UNROLL = 8


def _cparams(*sem, unchecked_dma=False):
    return pltpu.CompilerParams(dimension_semantics=sem, vmem_limit_bytes=VMEM_LIMIT,
                                disable_bounds_checks=unchecked_dma)


def _sigmoid(x):
    return 1.0 / (1.0 + jnp.exp(-x))


def _gelu(x):
    return 0.5 * x * (1.0 + jnp.tanh(math.sqrt(2.0 / math.pi) * (x + 0.044715 * (x * x * x))))


def _split3(x):
    hi = x.astype(BF16)
    r1 = x - hi.astype(F32)
    mid = r1.astype(BF16)
    lo = (r1 - mid.astype(F32)).astype(BF16)
    return hi, mid, lo


def _mod_kernel(c_ref, w_ref, b_ref, o_ref):
    cv = c_ref[...]
    s = (cv * _sigmoid(cv)).astype(BF16)
    o_ref[...] = jnp.dot(s, w_ref[...].astype(BF16), preferred_element_type=F32) + b_ref[...]


def _modulation(cvecs, mod_w, mod_b):
    depth, d, n = mod_w.shape
    tn = 512
    return pl.pallas_call(
        _mod_kernel,
        out_shape=jax.ShapeDtypeStruct((depth, 8, n), F32),
        grid=(depth, n // tn),
        in_specs=[pl.BlockSpec((8, d), lambda l, j: (0, 0)),
                  pl.BlockSpec((None, d, tn), lambda l, j: (l, 0, j)),
                  pl.BlockSpec((None, 1, tn), lambda l, j: (l, 0, j))],
        out_specs=pl.BlockSpec((None, 8, tn), lambda l, j: (l, 0, j)),
        compiler_params=_cparams("parallel", "parallel"),
        name="modulation",
    )(cvecs, mod_w, mod_b.reshape(depth, 1, n))


def _modulated_norm(x, g_ref, sh_ref, sc_ref):
    ms = jnp.mean(x * x, axis=-1, keepdims=True)
    return x * lax.rsqrt(ms + EPS) * g_ref[...] * (1.0 + sc_ref[...]) + sh_ref[...]


def _norm_kernel(x_ref, g_ref, sh_ref, sc_ref, o_ref):
    o_ref[...] = _modulated_norm(x_ref[...], g_ref, sh_ref, sc_ref).astype(BF16)


def _norm_modulate(xt, g, shifts, scales):
    t, d = xt.shape
    tm = ROW_TILE
    sel = lambda i: (jnp.minimum(i, 1), 0, 0)
    return pl.pallas_call(
        _norm_kernel,
        out_shape=jax.ShapeDtypeStruct((t, d), BF16),
        grid=(t // tm,),
        in_specs=[pl.BlockSpec((tm, d), lambda i: (i, 0)),
                  pl.BlockSpec((1, d), lambda i: (0, 0)),
                  pl.BlockSpec((None, 1, d), sel),
                  pl.BlockSpec((None, 1, d), sel)],
        out_specs=pl.BlockSpec((tm, d), lambda i: (i, 0)),
        compiler_params=_cparams("parallel"),
        name="norm_modulate",
    )(xt, g.reshape(1, d), shifts, scales)


def _inproj_kernel(h_ref, w_ref, o_ref, wbf_ref):
    @pl.when(pl.program_id(1) == 0)
    def _():
        wbf_ref[...] = w_ref[...].astype(BF16)

    o_ref[...] = jnp.dot(h_ref[...], wbf_ref[...], preferred_element_type=F32)


def _in_projection(h, w):
    t, d = h.shape
    n = w.shape[1]
    tm = 3 * ROW_TILE if t % (3 * ROW_TILE) == 0 else ROW_TILE
    tn = 512
    return pl.pallas_call(
        _inproj_kernel,
        out_shape=jax.ShapeDtypeStruct((t, n), F32),
        grid=(n // tn, t // tm),
        in_specs=[pl.BlockSpec((tm, d), lambda j, i: (i, 0)),
                  pl.BlockSpec((d, tn), lambda j, i: (0, j))],
        out_specs=pl.BlockSpec((tm, tn), lambda j, i: (i, j)),
        scratch_shapes=[pltpu.VMEM((d, tn), BF16)],
        compiler_params=_cparams("arbitrary", "arbitrary"),
        name="in_projection",
    )(h, w)


def _outproj_kernel(a1_ref, a2_ref, w_ref, res_ref, gate_ref, o_ref, wbf_ref, *, k1):
    @pl.when(pl.program_id(1) == 0)
    def _():
        wbf_ref[...] = w_ref[...].astype(BF16)

    acc = jnp.dot(a1_ref[...], wbf_ref[:k1, :], preferred_element_type=F32)
    acc = acc + jnp.dot(a2_ref[...], wbf_ref[k1:, :], preferred_element_type=F32)
    o_ref[...] = res_ref[...] + gate_ref[...] * acc


def _out_projection(a1, a2, w, res, gates, res_tile_off, gate_of_tile):
    r, k1 = a1.shape
    k2 = a2.shape[1]
    d = w.shape[1]
    tm, tn = ROW_TILE, 1024
    return pl.pallas_call(
        functools.partial(_outproj_kernel, k1=k1),
        out_shape=jax.ShapeDtypeStruct((r, d), F32),
        grid=(d // tn, r // tm),
        in_specs=[pl.BlockSpec((tm, k1), lambda j, i: (i, 0)),
                  pl.BlockSpec((tm, k2), lambda j, i: (i, 0)),
                  pl.BlockSpec((k1 + k2, tn), lambda j, i: (0, j)),
                  pl.BlockSpec((tm, tn), lambda j, i: (i + res_tile_off, j)),
                  pl.BlockSpec((None, 1, tn), lambda j, i: (gate_of_tile(i), 0, j))],
        out_specs=pl.BlockSpec((tm, tn), lambda j, i: (i, j)),
        scratch_shapes=[pltpu.VMEM((k1 + k2, tn), BF16)],
        compiler_params=_cparams("arbitrary", "arbitrary"),
        name="out_projection",
    )(a1, a2, w, res, gates)


def _qkv_kernel(q_ref, k_ref, v_ref, cos_ref, sa_ref, sb_ref, qg_ref, kg_ref, qo_ref, ko_ref, vo_ref):
    cos, sa, sb = cos_ref[...], sa_ref[...], sb_ref[...]

    def head(x, g, scale):
        ms = jnp.mean(x * x, axis=-1, keepdims=True)
        xn = x * lax.rsqrt(ms + EPS) * g
        r = xn * cos + pltpu.roll(xn, HEAD_DIM - ROPE_PAIRS, 1) * sa + pltpu.roll(xn, ROPE_PAIRS, 1) * sb
        return (r * scale).astype(BF16)

    for h in range(Q_HEADS):
        sl = slice(h * HEAD_DIM, (h + 1) * HEAD_DIM)
        qo_ref[:, sl] = head(q_ref[:, sl], qg_ref[...], HEAD_DIM ** -0.5)
    for h in range(KV_HEADS):
        sl = slice(h * HEAD_DIM, (h + 1) * HEAD_DIM)
        ko_ref[:, sl] = head(k_ref[:, sl], kg_ref[...], 1.0)
    vo_ref[...] = v_ref[...].astype(BF16)


def _rope_tables(n_ctx, n_lat):
    t = jnp.arange(n_lat)
    row = (t // GRID_W).astype(F32)
    col = (t % GRID_W).astype(F32)
    inv_freq = ROPE_BASE ** (-jnp.arange(ROPE_PAIRS, dtype=F32) / ROPE_PAIRS)
    ang_r = row[:, None] * inv_freq
    ang_c = col[:, None] * inv_freq
    ang = jnp.concatenate([ang_r, ang_r, ang_c, ang_c], axis=-1)
    ang = jnp.concatenate([jnp.zeros((n_ctx, HEAD_DIM), F32), ang], axis=0)
    cos, sin = jnp.cos(ang), jnp.sin(ang)
    first = (jnp.arange(HEAD_DIM) % (2 * ROPE_PAIRS)) < ROPE_PAIRS
    return cos, jnp.where(first, -sin, 0.0), jnp.where(first, 0.0, sin)


def _qkv_prepare(proj, q_g, k_g, n_ctx):
    t = proj.shape[0]
    tm = ROW_TILE
    qw, kw = Q_HEADS * HEAD_DIM, KV_HEADS * HEAD_DIM
    cos, sa, sb = _rope_tables(n_ctx, t - n_ctx)
    tab = pl.BlockSpec((tm, HEAD_DIM), lambda i: (i, 0))
    vec = pl.BlockSpec((1, HEAD_DIM), lambda i: (0, 0))
    return pl.pallas_call(
        _qkv_kernel,
        out_shape=(jax.ShapeDtypeStruct((t, qw), BF16), jax.ShapeDtypeStruct((t, kw), BF16),
                   jax.ShapeDtypeStruct((t, kw), BF16)),
        grid=(t // tm,),
        in_specs=[pl.BlockSpec((tm, qw), lambda i: (i, 0)),
                  pl.BlockSpec((tm, kw), lambda i: (i, qw // kw)),
                  pl.BlockSpec((tm, kw), lambda i: (i, qw // kw + 1)),
                  tab, tab, tab, vec, vec],
        out_specs=(pl.BlockSpec((tm, qw), lambda i: (i, 0)), pl.BlockSpec((tm, kw), lambda i: (i, 0)),
                   pl.BlockSpec((tm, kw), lambda i: (i, 0))),
        compiler_params=_cparams("parallel"),
        name="qkv_prepare",
    )(proj, proj, proj, cos, sa, sb, q_g.reshape(1, HEAD_DIM), k_g.reshape(1, HEAD_DIM))


def _attn_kernel(sink_ref, q_ref, kp_ref, kc_ref, kn_ref, kx_ref, vp_ref, vc_ref, vn_ref, vx_ref, o_ref,
                 *, ctx_blocks, n_blocks):
    b = pl.program_id(0)
    blk = ATT_BLOCK
    rows = Q_PER_KV * blk
    n_ctx = ctx_blocks * blk
    latent = b >= ctx_blocks
    lat_f = jnp.where(latent, 1.0, 0.0)
    prev_f = jnp.where(jnp.logical_and(latent, b - 1 >= ctx_blocks), 1.0, 0.0)
    next_f = jnp.where(jnp.logical_and(latent, b + 1 <= n_blocks - 1), 1.0, 0.0)
    qi = lax.broadcasted_iota(jnp.int32, (rows, 3 * blk + n_ctx), 0) % blk
    col = lax.broadcasted_iota(jnp.int32, (rows, 3 * blk + n_ctx), 1)
    kj = col % blk
    allowed = jnp.where(col < blk, jnp.where(kj >= qi, prev_f, 0.0),
                        jnp.where(col < 2 * blk, lat_f,
                                  jnp.where(col < 3 * blk, jnp.where(kj <= qi, next_f, 0.0), 1.0))) > 0.5
    for h in range(KV_HEADS):
        hs = slice(h * HEAD_DIM, (h + 1) * HEAD_DIM)
        q4 = jnp.concatenate([q_ref[:, (Q_PER_KV * h + g) * HEAD_DIM:(Q_PER_KV * h + g + 1) * HEAD_DIM]
                              for g in range(Q_PER_KV)], axis=0)
        kk = jnp.concatenate([kp_ref[:, hs], kc_ref[:, hs], kn_ref[:, hs], kx_ref[:, hs]], axis=0)
        vv = jnp.concatenate([vp_ref[:, hs], vc_ref[:, hs], vn_ref[:, hs], vx_ref[:, hs]], axis=0)
        s = lax.dot_general(q4, kk, (((1,), (1,)), ((), ())), preferred_element_type=F32)
        s = jnp.where(allowed, s, NEG_INF)
        sk = jnp.concatenate([jnp.full((blk, 1), sink_ref[Q_PER_KV * h + g], F32) for g in range(Q_PER_KV)],
                             axis=0)
        m = jnp.maximum(jnp.max(s, axis=-1, keepdims=True), sk)
        p = jnp.exp(s - m)
        den = jnp.sum(p, axis=-1, keepdims=True) + jnp.exp(sk - m)
        o = jnp.dot(p.astype(BF16), vv, preferred_element_type=F32) / den
        for g in range(Q_PER_KV):
            o_ref[:, (Q_PER_KV * h + g) * HEAD_DIM:(Q_PER_KV * h + g + 1) * HEAD_DIM] = (
                o[g * blk:(g + 1) * blk, :].astype(BF16))


def _attention(q, k, v, sink, n_ctx):
    t = q.shape[0]
    blk = ATT_BLOCK
    nb = t // blk
    cb = n_ctx // blk
    kw = k.shape[1]
    prev = pl.BlockSpec((blk, kw), lambda b, s: (jnp.maximum(b - 1, 0), 0))
    cur = pl.BlockSpec((blk, kw), lambda b, s: (b, 0))
    nxt = pl.BlockSpec((blk, kw), lambda b, s: (jnp.minimum(b + 1, nb - 1), 0))
    ctx = pl.BlockSpec((n_ctx, kw), lambda b, s: (0, 0))
    return pl.pallas_call(
        functools.partial(_attn_kernel, ctx_blocks=cb, n_blocks=nb),
        out_shape=jax.ShapeDtypeStruct(q.shape, BF16),
        grid_spec=pltpu.PrefetchScalarGridSpec(
            num_scalar_prefetch=1, grid=(nb,),
            in_specs=[pl.BlockSpec((blk, q.shape[1]), lambda b, s: (b, 0)),
                      prev, cur, nxt, ctx, prev, cur, nxt, ctx],
            out_specs=pl.BlockSpec((blk, q.shape[1]), lambda b, s: (b, 0))),
        compiler_params=_cparams("parallel"),
        name="window_attention",
    )(sink, q, k, k, k, k, v, v, v, v)


def _pool_kernel(*refs, n_ctx, n_lat, n_tiles):
    ng = len(POOL_WINDOWS)
    cur_refs, prev_refs, next_refs = refs[:ng], refs[ng:2 * ng], refs[2 * ng:3 * ng]
    w_ref, scale_ref, o_ref, xe_ref = refs[3 * ng:]
    i = pl.program_id(0)
    tm = ROW_TILE
    hl = POOL_HALO
    gw = w_ref.shape[1]
    prev_valid = i >= 2
    next_valid = jnp.logical_and(i >= 1, i < n_tiles - 1)
    pos = lax.broadcasted_iota(jnp.int32, (tm, 1), 0) + jnp.where(i == 0, 0, (i - 1) * tm)
    seg_len = jnp.where(i == 0, n_ctx, n_lat)
    for g, win in enumerate(POOL_WINDOWS):
        cur = cur_refs[g][...]
        xe_ref[0:hl, :] = jnp.where(prev_valid, prev_refs[g][...], 0.0)
        xe_ref[hl:hl + tm, :] = cur
        xe_ref[hl + tm:hl + tm + hl, :] = jnp.where(next_valid, next_refs[g][...], 0.0)
        acc = xe_ref[hl - win // 2:hl - win // 2 + tm, :]
        for o in range(1 - win // 2, win // 2):
            acc = acc + xe_ref[hl + o:hl + o + tm, :]
        cnt = jnp.minimum(pos - win // 2 + win, seg_len) - jnp.maximum(pos - win // 2, 0)
        d = acc / cnt.astype(F32) - cur
        y = jnp.dot(d.astype(BF16), w_ref[g].astype(BF16), preferred_element_type=F32)
        o_ref[:, g * gw:(g + 1) * gw] = (y * scale_ref[:, g * gw:(g + 1) * gw]).astype(BF16)


def _pool_mixer(proj, col0, w_pool, pool_scale, n_ctx):
    t = proj.shape[0]
    tm, hl = ROW_TILE, POOL_HALO
    ng, gw = w_pool.shape[0], w_pool.shape[1]
    nt = t // tm
    cb = col0 // gw
    per = tm // hl
    cur = [pl.BlockSpec((tm, gw), functools.partial(lambda i, g: (i, cb + g), g=g)) for g in range(ng)]
    prev = [pl.BlockSpec((hl, gw), functools.partial(lambda i, g: (jnp.maximum(i * per - 1, 0), cb + g), g=g))
            for g in range(ng)]
    nxt = [pl.BlockSpec((hl, gw), functools.partial(lambda i, g: (jnp.minimum((i + 1) * per, nt * per - 1), cb + g), g=g))
           for g in range(ng)]
    return pl.pallas_call(
        functools.partial(_pool_kernel, n_ctx=n_ctx, n_lat=t - n_ctx, n_tiles=nt),
        out_shape=jax.ShapeDtypeStruct((t, ng * gw), BF16),
        grid=(nt,),
        in_specs=cur + prev + nxt + [pl.BlockSpec((ng, gw, gw), lambda i: (0, 0, 0)),
                                     pl.BlockSpec((1, ng * gw), lambda i: (0, 0))],
        out_specs=pl.BlockSpec((tm, ng * gw), lambda i: (i, 0)),
        scratch_shapes=[pltpu.VMEM((tm + 2 * hl, gw), F32)],
        compiler_params=_cparams("parallel"),
        name="pool_mixer",
    )(*([proj] * (3 * ng)), w_pool, pool_scale.reshape(1, ng * gw))


def _s5_powers(lr, li, dt, kcol):
    mag = jnp.exp(kcol * (lr * dt))
    ang = kcol * (li * dt)
    return mag * jnp.cos(ang), mag * jnp.sin(ang)


def _s5_expand(z_re, z_im, e_re, e_im):
    n, p = e_re.shape
    hh = z_re.shape[0]
    er = jnp.broadcast_to(e_re[:, None, :], (n, hh, p)).reshape(n * hh, p)
    ei = jnp.broadcast_to(e_im[:, None, :], (n, hh, p)).reshape(n * hh, p)
    zr = jnp.broadcast_to(z_re[None, :, :], (n, hh, p)).reshape(n * hh, p)
    zi = jnp.broadcast_to(z_im[None, :, :], (n, hh, p)).reshape(n * hh, p)
    return zr * er - zi * ei, zr * ei + zi * er


def _s5_discretize_t(lr, li, dt, bt_re, bt_im):
    mag = jnp.exp(lr * dt)
    a_re, a_im = mag * jnp.cos(li * dt), mag * jnp.sin(li * dt)
    den = lr * lr + li * li
    n_re = a_re - 1.0
    k_re = (n_re * lr + a_im * li) / den
    k_im = (a_im * lr - n_re * li) / den
    return k_re * bt_re - k_im * bt_im, k_re * bt_im + k_im * bt_re


def _s5_group_params(lam_re_ref, lam_im_ref, ldt_ref, bt_re_ref, bt_im_ref, d):
    lr, li = lam_re_ref[d], lam_im_ref[d]
    dt = jnp.exp(ldt_ref[d])
    bb_re, bb_im = _s5_discretize_t(lr, li, dt, bt_re_ref[d], bt_im_ref[d])
    return lr, li, dt, bb_re, bb_im


def _s5_state_kernel(u_ref, lam_re_ref, lam_im_ref, ldt_ref, bt_re_ref, bt_im_ref, sf_ref, sb_ref):
    lc = S5_CHUNK
    u = u_ref[...].astype(BF16)
    t_col = lax.broadcasted_iota(jnp.int32, (lc, 1), 0).astype(F32)
    for d, out_ref, kcol in ((0, sf_ref, (lc - 1) - t_col), (1, sb_ref, t_col)):
        lr, li, dt, bb_re, bb_im = _s5_group_params(lam_re_ref, lam_im_ref, ldt_ref, bt_re_ref, bt_im_ref, d)
        e_re, e_im = _s5_powers(lr, li, dt, kcol)
        w_re, w_im = _s5_expand(bb_re, bb_im, e_re, e_im)
        w = jnp.concatenate([w_re, w_im], axis=1).astype(BF16)
        out_ref[...] = jnp.dot(u, w, preferred_element_type=F32)


def _s5_carry_kernel(sf_ref, sb_ref, lam_re_ref, lam_im_ref, ldt_ref, hf_ref, hb_ref, sw_ref, *, ctx_chunks):
    nc = sf_ref.shape[0]
    lanes = sf_ref.shape[1]
    first = (lax.broadcasted_iota(jnp.int32, (1, lanes), 1) % (2 * S5_P)) < S5_P

    def scan(d, s_ref, h_ref, order):
        lr, li, dt = lam_re_ref[d], lam_im_ref[d], jnp.exp(ldt_ref[d])
        mag = jnp.exp(S5_CHUNK * (lr * dt))
        ang = S5_CHUNK * (li * dt)
        pa = mag * jnp.cos(ang)
        a_im = mag * jnp.sin(ang)
        qa = jnp.where(first, -a_im, a_im)
        s_all = s_ref[...]
        sw_ref[...] = jnp.where(first, pltpu.roll(s_all, lanes - S5_P, 1), pltpu.roll(s_all, S5_P, 1))
        zero = jnp.zeros((1, lanes), F32)

        def run(lo, n, step, carry):
            def body(k, hc):
                h, hs = hc
                c = lo + k * step
                h_ref[pl.ds(c, 1), :] = h
                s = s_ref[pl.ds(c, 1), :]
                ss = sw_ref[pl.ds(c, 1), :]
                return pa * h + qa * hs + s, pa * hs - qa * h + ss
            return lax.fori_loop(0, n, body, carry)

        carry = (zero, zero)
        for lo, n, step in order:
            carry = run(lo, n, step, carry)

    scan(0, sf_ref, hf_ref, ((0, nc, 1),))
    scan(1, sb_ref, hb_ref, ((ctx_chunks - 1, ctx_chunks, -1), (nc - 1, nc - ctx_chunks, -1)))


def _s5_output_kernel(u_ref, hf_ref, hb_ref, lam_re_ref, lam_im_ref, ldt_ref, bt_re_ref, bt_im_ref,
                      c_re_ref, c_im_ref, y_ref):
    lc = S5_CHUNK
    half = lc // 2
    n = lc * S5_H
    u = u_ref[...].astype(BF16)
    t_col = lax.broadcasted_iota(jnp.int32, (lc, 1), 0).astype(F32)
    s_idx = lax.broadcasted_iota(jnp.int32, (n, n), 0) // S5_H
    t_idx = lax.broadcasted_iota(jnp.int32, (n, n), 1) // S5_H
    mt = jnp.zeros((n, n), F32)
    y = jnp.zeros((u.shape[0], n), F32)
    for d, h_ref in ((0, hf_ref), (1, hb_ref)):
        lr, li, dt, bb_re, bb_im = _s5_group_params(lam_re_ref, lam_im_ref, ldt_ref, bt_re_ref, bt_im_ref, d)
        c_re, c_im = c_re_ref[d], c_im_ref[d]
        if d == 0:
            kl, kr, kc = t_col - half, half - t_col, t_col + 1.0
            keep = s_idx <= t_idx
        else:
            kl, kr, kc = half - t_col, t_col - half, lc - t_col
            keep = s_idx >= t_idx
        l_re, l_im = _s5_expand(c_re, c_im, *_s5_powers(lr, li, dt, kl))
        r_re, r_im = _s5_expand(bb_re, bb_im, *_s5_powers(lr, li, dt, kr))
        lst = jnp.concatenate([l_re, -l_im], axis=1)
        rst = jnp.concatenate([r_re, r_im], axis=1)
        r3, l3 = _split3(rst), _split3(lst)
        m = jnp.zeros((n, n), F32)
        for ra, la in ((0, 0), (0, 1), (1, 0), (0, 2), (1, 1), (2, 0)):
            m = m + lax.dot_general(r3[ra], l3[la], (((1,), (1,)), ((), ())), preferred_element_type=F32)
        mt = mt + jnp.where(keep, m, 0.0)
        ca_re, ca_im = _s5_expand(c_re, c_im, *_s5_powers(lr, li, dt, kc))
        t2 = jnp.concatenate([ca_re, -ca_im], axis=1).astype(BF16)
        y = y + lax.dot_general(h_ref[...].astype(BF16), t2, (((1,), (1,)), ((), ())),
                                preferred_element_type=F32)
    y_ref[...] = y + jnp.dot(u, mt.astype(BF16), preferred_element_type=F32)


def _s5_scan(s_all, n_ctx, lam_re, lam_im, log_dt, b_re, b_im, c_re, c_im):
    t, cw = s_all.shape
    g, p, h, lc = cw // S5_H, S5_P, S5_H, S5_CHUNK
    nc = t // lc
    ug = s_all.reshape(nc, lc, g, h).transpose(2, 0, 1, 3).reshape(g, nc, lc * h)
    lam_re4 = lam_re.transpose(1, 0, 2)[:, :, None, :]
    lam_im4 = lam_im.transpose(1, 0, 2)[:, :, None, :]
    ldt4 = jnp.broadcast_to(log_dt.T[:, :, None, None], (g, 2, 1, p))
    bt_re = b_re.transpose(1, 0, 3, 2)
    bt_im = b_im.transpose(1, 0, 3, 2)
    c_re4 = c_re.transpose(1, 0, 2, 3)
    c_im4 = c_im.transpose(1, 0, 2, 3)
    vec4 = pl.BlockSpec((None, 2, 1, p), lambda i: (i, 0, 0, 0))
    mat4 = pl.BlockSpec((None, 2, h, p), lambda i: (i, 0, 0, 0))
    u_spec = pl.BlockSpec((None, nc, lc * h), lambda i: (i, 0, 0))
    st_spec = pl.BlockSpec((nc, 2 * p), lambda i: (0, i))
    st_shape = jax.ShapeDtypeStruct((nc, g * 2 * p), F32)
    s_f, s_b = pl.pallas_call(
        _s5_state_kernel, out_shape=(st_shape, st_shape), grid=(g,),
        in_specs=[u_spec, vec4, vec4, vec4, mat4, mat4],
        out_specs=(st_spec, st_spec),
        compiler_params=_cparams("parallel"), name="s5_chunk_states",
    )(ug, lam_re4, lam_im4, ldt4, bt_re, bt_im)

    def lanes(x):
        return jnp.concatenate([x, x], axis=-1).reshape(2, 1, g * 2 * p)
    cols = 8 * 2 * p
    row = pl.BlockSpec((2, 1, cols), lambda j: (0, 0, j))
    blk = pl.BlockSpec((nc, cols), lambda j: (0, j))
    h_f, h_b = pl.pallas_call(
        functools.partial(_s5_carry_kernel, ctx_chunks=n_ctx // lc),
        out_shape=(st_shape, st_shape), grid=(g * 2 * p // cols,),
        in_specs=[blk, blk, row, row, row], out_specs=(blk, blk),
        scratch_shapes=[pltpu.VMEM((nc, cols), F32)],
        compiler_params=_cparams("parallel"), name="s5_chunk_carry",
    )(s_f, s_b, lanes(lam_re), lanes(lam_im), lanes(jnp.broadcast_to(log_dt[:, :, None], (2, g, p))))

    yg = pl.pallas_call(
        _s5_output_kernel, out_shape=jax.ShapeDtypeStruct((g, nc, lc * h), F32), grid=(g,),
        in_specs=[u_spec, st_spec, st_spec, vec4, vec4, vec4, mat4, mat4, mat4, mat4],
        out_specs=u_spec,
        compiler_params=_cparams("parallel"), name="s5_outputs",
    )(ug, h_f, h_b, lam_re4, lam_im4, ldt4, bt_re, bt_im, c_re4, c_im4)
    return yg.reshape(g, nc, lc, h).transpose(1, 2, 0, 3).reshape(t, cw)


def _glu_kernel(y_ref, u_ref, dk_ref, w_ref, b_ref, o_ref, wbf_ref):
    @pl.when(pl.program_id(0) == 0)
    def _():
        wbf_ref[...] = w_ref[...].astype(BF16)

    yl = dk_ref[...] * u_ref[...] + y_ref[...]
    gl = _gelu(yl)
    z = jnp.dot(gl.astype(BF16), wbf_ref[...], preferred_element_type=F32) + b_ref[...]
    o_ref[...] = (gl * _sigmoid(z)).astype(BF16)


def _s5_glu(y_scan, proj, d_skip, glu_w, glu_b, n_ctx):
    t, cw = y_scan.shape
    tm = ROW_TILE
    off = n_ctx // tm
    n_lat = t - n_ctx
    vec = pl.BlockSpec((1, cw), lambda i: (0, 0))
    return pl.pallas_call(
        _glu_kernel, out_shape=jax.ShapeDtypeStruct((n_lat, cw), BF16), grid=(n_lat // tm,),
        in_specs=[pl.BlockSpec((tm, cw), lambda i: (i + off, 0)),
                  pl.BlockSpec((tm, cw), lambda i: (i + off, 0)),
                  vec, pl.BlockSpec((cw, cw), lambda i: (0, 0)), vec],
        out_specs=pl.BlockSpec((tm, cw), lambda i: (i, 0)),
        scratch_shapes=[pltpu.VMEM((cw, cw), BF16)],
        compiler_params=_cparams("arbitrary"), name="s5_glu",
    )(y_scan, proj, d_skip.reshape(1, cw), glu_w, glu_b.reshape(1, cw))


def _gmlp_kernel(*refs, n_col):
    u_refs, v_refs = refs[:n_col], refs[n_col:2 * n_col]
    lng_ref, lnb_ref, ws_ref, bst_ref, o_ref = refs[2 * n_col:]
    u = _gelu(jnp.concatenate([r[...] for r in u_refs], axis=1))
    v = _gelu(jnp.concatenate([r[...] for r in v_refs], axis=1))
    mu = jnp.mean(v, axis=-1, keepdims=True)
    vc = v - mu
    var = jnp.mean(vc * vc, axis=-1, keepdims=True)
    vn = (vc * lax.rsqrt(var + EPS) * lng_ref[...] + lnb_ref[...]).astype(BF16)
    gw = u.shape[1] // SG_HEADS
    for c in range(u.shape[0] // SG_CHUNK):
        rs = slice(c * SG_CHUNK, (c + 1) * SG_CHUNK)
        for g in range(SG_HEADS):
            cs = slice(g * gw, (g + 1) * gw)
            mixed = jnp.dot(ws_ref[g].astype(BF16), vn[rs, cs], preferred_element_type=F32)
            mixed = mixed + bst_ref[:, g:g + 1]
            o_ref[rs, cs] = (u[rs, cs] * mixed).astype(BF16)


def _gmlp_mixer(proj, col_u, col_v, width, ln_g, ln_b, w_s, b_s, n_ctx):
    t = proj.shape[0]
    tm, cwid = ROW_TILE, 1024
    off = n_ctx // tm
    n_lat = t - n_ctx
    n_col = width // cwid
    specs = [pl.BlockSpec((tm, cwid), functools.partial(lambda i, cb: (i + off, cb), cb=(c0 // cwid) + k))
             for c0 in (col_u, col_v) for k in range(n_col)]
    vec = pl.BlockSpec((1, width), lambda i: (0, 0))
    return pl.pallas_call(
        functools.partial(_gmlp_kernel, n_col=n_col),
        out_shape=jax.ShapeDtypeStruct((n_lat, width), BF16), grid=(n_lat // tm,),
        in_specs=specs + [vec, vec, pl.BlockSpec(w_s.shape, lambda i: (0, 0, 0)),
                          pl.BlockSpec((SG_CHUNK, SG_HEADS), lambda i: (0, 0))],
        out_specs=pl.BlockSpec((tm, width), lambda i: (i, 0)),
        compiler_params=_cparams("parallel"), name="gmlp_mixer",
    )(*([proj] * (2 * n_col)), ln_g.reshape(1, width), ln_b.reshape(1, width), w_s, b_s.T)


def _router_kernel(x_ref, g_ref, sh_ref, sc_ref, wr_ref, br_ref, hp_ref, rt_ref, cnt_ref, w3_ref, run_ref):
    @pl.when(pl.program_id(0) == 0)
    def _():
        hi, mid, lo = _split3(wr_ref[...])
        w3_ref[0], w3_ref[1], w3_ref[2] = hi, mid, lo
        run_ref[...] = jnp.zeros_like(run_ref)

    h = _modulated_norm(x_ref[...], g_ref, sh_ref, sc_ref)
    tm = h.shape[0]
    half = h.shape[1] // 2
    packed = pltpu.pack_elementwise([h[:, :half], h[:, half:]], packed_dtype=jnp.bfloat16)
    slab = half // LANES
    for s in range(slab):
        hp_ref[pl.ds(s, tm, stride=slab), :] = packed[:, s * LANES:(s + 1) * LANES]

    h3 = _split3(h)
    logits = br_ref[...]
    for a, b in ((0, 0), (0, 1), (1, 0), (0, 2), (1, 1), (2, 0)):
        logits = logits + jnp.dot(h3[a], w3_ref[b], preferred_element_type=F32)

    lane = lax.broadcasted_iota(jnp.int32, logits.shape, 1)
    big = jnp.int32(ROUTER_LANES)
    gl = jnp.where(lane < MOE_GROUPS, logits, NEG_INF)
    gm = jnp.max(gl, axis=-1, keepdims=True)
    g_w = 1.0 / jnp.sum(jnp.exp(gl - gm), axis=-1, keepdims=True)
    g_idx = jnp.min(jnp.where(gl == gm, lane, big), axis=-1, keepdims=True)
    lo_lane = MOE_GROUPS + g_idx * MOE_PER_GROUP
    fl = jnp.where(jnp.logical_and(lane >= lo_lane, lane < lo_lane + MOE_PER_GROUP), logits, NEG_INF)
    m1 = jnp.max(fl, axis=-1, keepdims=True)
    i1 = jnp.min(jnp.where(fl == m1, lane, big), axis=-1, keepdims=True)
    fl2 = jnp.where(lane == i1, NEG_INF, fl)
    m2 = jnp.max(fl2, axis=-1, keepdims=True)
    i2 = jnp.min(jnp.where(fl2 == m2, lane, big), axis=-1, keepdims=True)
    e2 = jnp.exp(m2 - m1)
    w1 = g_w / (1.0 + e2)
    w2 = g_w * e2 / (1.0 + e2)
    oh1 = jnp.where(lane == i1 - MOE_GROUPS, 1.0, 0.0)
    oh2 = jnp.where(lane == i2 - MOE_GROUPS, 1.0, 0.0)
    both = oh1 + oh2
    tri = jnp.where(lax.broadcasted_iota(jnp.int32, (tm, tm), 1) < lax.broadcasted_iota(jnp.int32, (tm, tm), 0),
                    1.0, 0.0).astype(BF16)
    before = jnp.dot(tri, both.astype(BF16), preferred_element_type=F32) + run_ref[...]
    r1 = jnp.sum(before * oh1, axis=-1, keepdims=True)
    r2 = jnp.sum(before * oh2, axis=-1, keepdims=True)
    run_ref[...] = run_ref[...] + jnp.sum(both, axis=0, keepdims=True)
    cnt_ref[...] = jnp.broadcast_to(run_ref[...], cnt_ref.shape)
    rt_ref[...] = jnp.where(lane == 0, (i1 - MOE_GROUPS).astype(F32),
                            jnp.where(lane == 1, (i2 - MOE_GROUPS).astype(F32),
                                      jnp.where(lane == 2, w1,
                                                jnp.where(lane == 3, w2,
                                                          jnp.where(lane == 4, r1, jnp.where(lane == 5, r2, 0.0))))))


def _moe_router(xt, tile_off, n_rows, g, shifts, scales, first_latent_tile, w_grp, b_grp, w_rt, b_rt):
    d = xt.shape[1]
    tm = ROW_TILE
    slab = d // 2 // LANES
    ne = MOE_GROUPS * MOE_PER_GROUP
    wr = jnp.concatenate([w_grp, w_rt.transpose(1, 0, 2).reshape(d, ne),
                          jnp.zeros((d, ROUTER_LANES - MOE_GROUPS - ne), F32)], axis=1)
    br = jnp.concatenate([b_grp, b_rt.reshape(ne), jnp.zeros((ROUTER_LANES - MOE_GROUPS - ne,), F32)])
    sel = lambda i: (jnp.where(i + tile_off >= first_latent_tile, 1, 0), 0, 0)
    return pl.pallas_call(
        _router_kernel,
        out_shape=(jax.ShapeDtypeStruct((n_rows * slab, LANES), jnp.uint32),
                   jax.ShapeDtypeStruct((n_rows, ROUTER_LANES), F32),
                   jax.ShapeDtypeStruct((8, ROUTER_LANES), F32)),
        grid=(n_rows // tm,),
        in_specs=[pl.BlockSpec((tm, d), lambda i: (i + tile_off, 0)),
                  pl.BlockSpec((1, d), lambda i: (0, 0)),
                  pl.BlockSpec((None, 1, d), sel), pl.BlockSpec((None, 1, d), sel),
                  pl.BlockSpec((d, ROUTER_LANES), lambda i: (0, 0)),
                  pl.BlockSpec((1, ROUTER_LANES), lambda i: (0, 0))],
        out_specs=(pl.BlockSpec((tm * slab, LANES), lambda i: (i, 0)),
                   pl.BlockSpec((tm, ROUTER_LANES), lambda i: (i, 0)),
                   pl.BlockSpec((8, ROUTER_LANES), lambda i: (0, 0))),
        scratch_shapes=[pltpu.VMEM((3, d, ROUTER_LANES), BF16), pltpu.VMEM((1, ROUTER_LANES), F32)],
        compiler_params=_cparams("arbitrary"), name="moe_router",
    )(xt, g.reshape(1, d), shifts, scales, wr, br.reshape(1, ROUTER_LANES))


def _row_scatter_kernel(pos_ref, hp_ref, init_hbm, out_hbm, sem, *, slab):
    del init_hbm
    i = pl.program_id(0)
    tm = hp_ref.shape[0] // slab

    def issue(r, carry):
        src = pl.multiple_of(r * slab, slab)
        for k in range(2):
            dst = pl.multiple_of(pos_ref[2 * (i * tm + r) + k] * slab, slab)
            pltpu.make_async_copy(hp_ref.at[pl.ds(src, slab)], out_hbm.at[pl.ds(dst, slab)], sem).start()
        return carry

    lax.fori_loop(0, tm, issue, 0, unroll=DMA_ISSUE_UNROLL)
    for _ in range(2):
        pltpu.make_async_copy(hp_ref, out_hbm.at[pl.ds(0, tm * slab)], sem).wait()


def _moe_dispatch(hp, slab, pos, n_tiles):
    tm = ROW_TILE
    n_rows = hp.shape[0] // slab
    zeros = jnp.zeros((n_tiles * MOE_TILE * slab, LANES), hp.dtype)
    return pl.pallas_call(
        functools.partial(_row_scatter_kernel, slab=slab),
        out_shape=jax.ShapeDtypeStruct(zeros.shape, hp.dtype),
        grid_spec=pltpu.PrefetchScalarGridSpec(
            num_scalar_prefetch=1, grid=(n_rows // tm,),
            in_specs=[pl.BlockSpec((tm * slab, LANES), lambda i, p: (i, 0)),
                      pl.BlockSpec(memory_space=pl.ANY)],
            out_specs=pl.BlockSpec(memory_space=pl.ANY),
            scratch_shapes=[pltpu.SemaphoreType.DMA(())]),
        input_output_aliases={2: 0},
        compiler_params=_cparams("arbitrary", unchecked_dma=True), name="moe_dispatch",
    )(pos, hp, zeros)


def _unpack_pairs(xp):
    lo = pltpu.unpack_elementwise(xp, index=0, packed_dtype=jnp.bfloat16, unpacked_dtype=F32)
    hi = pltpu.unpack_elementwise(xp, index=1, packed_dtype=jnp.bfloat16, unpacked_dtype=F32)
    return lo.astype(BF16), hi.astype(BF16)


def _expert_up_kernel(te_ref, nu_ref, xs_ref, wg_ref, wu_ref, h_ref, wgb_ref, wub_ref):
    t = pl.program_id(0)
    tc = jnp.minimum(t, nu_ref[0] - 1)
    new_expert = jnp.logical_or(t == 0, te_ref[tc] != te_ref[jnp.maximum(tc - 1, 0)])

    @pl.when(t < nu_ref[0])
    def _():
        @pl.when(new_expert)
        def _():
            wgb_ref[...] = wg_ref[...].astype(BF16)
            wub_ref[...] = wu_ref[...].astype(BF16)

        tm = h_ref.shape[0]
        slab = xs_ref.shape[0] // tm
        parts = [_unpack_pairs(xs_ref[pl.ds(s, tm, stride=slab), :]) for s in range(slab)]
        lo = jnp.concatenate([p[0] for p in parts], axis=1)
        hi = jnp.concatenate([p[1] for p in parts], axis=1)
        half = lo.shape[1]
        gate = (jnp.dot(lo, wgb_ref[:half, :], preferred_element_type=F32)
                + jnp.dot(hi, wgb_ref[half:, :], preferred_element_type=F32))
        up = (jnp.dot(lo, wub_ref[:half, :], preferred_element_type=F32)
              + jnp.dot(hi, wub_ref[half:, :], preferred_element_type=F32))
        h_ref[...] = (gate * _sigmoid(gate) * up).astype(BF16)


def _expert_down_kernel(te_ref, nu_ref, h_ref, wd_ref, y_ref, wdb_ref):
    t = pl.program_id(0)
    tc = jnp.minimum(t, nu_ref[0] - 1)
    new_expert = jnp.logical_or(t == 0, te_ref[tc] != te_ref[jnp.maximum(tc - 1, 0)])

    @pl.when(t < nu_ref[0])
    def _():
        @pl.when(new_expert)
        def _():
            wdb_ref[...] = wd_ref[...].astype(BF16)

        y_ref[...] = jnp.dot(h_ref[...], wdb_ref[...], preferred_element_type=F32)


def _moe_experts(xs, tile_expert, n_used, layer, w_gate, w_up, w_down):
    tm = MOE_TILE
    d, ff = w_gate.shape[2], w_gate.shape[3]
    in_slab = d // 2 // LANES
    npad = xs.shape[0] // in_slab
    nt = npad // tm
    tile = lambda t, nu: jnp.minimum(t, nu[0] - 1)
    hmid = pl.pallas_call(
        _expert_up_kernel,
        out_shape=jax.ShapeDtypeStruct((npad, ff), BF16),
        grid_spec=pltpu.PrefetchScalarGridSpec(
            num_scalar_prefetch=2, grid=(nt,),
            in_specs=[pl.BlockSpec((tm * in_slab, LANES), lambda t, te, nu: (tile(t, nu), 0)),
                      pl.BlockSpec((None, None, d, ff), lambda t, te, nu: (layer, te[tile(t, nu)], 0, 0)),
                      pl.BlockSpec((None, None, d, ff), lambda t, te, nu: (layer, te[tile(t, nu)], 0, 0))],
            out_specs=pl.BlockSpec((tm, ff), lambda t, te, nu: (tile(t, nu), 0)),
            scratch_shapes=[pltpu.VMEM((d, ff), BF16), pltpu.VMEM((d, ff), BF16)]),
        compiler_params=_cparams("arbitrary"), name="moe_expert_up",
    )(tile_expert, n_used, xs, w_gate, w_up)
    return pl.pallas_call(
        _expert_down_kernel,
        out_shape=jax.ShapeDtypeStruct((npad, d), F32),
        grid_spec=pltpu.PrefetchScalarGridSpec(
            num_scalar_prefetch=2, grid=(nt,),
            in_specs=[pl.BlockSpec((tm, ff), lambda t, te, nu: (tile(t, nu), 0)),
                      pl.BlockSpec((None, None, ff, d), lambda t, te, nu: (layer, te[tile(t, nu)], 0, 0))],
            out_specs=pl.BlockSpec((tm, d), lambda t, te, nu: (tile(t, nu), 0)),
            scratch_shapes=[pltpu.VMEM((ff, d), BF16)]),
        compiler_params=_cparams("arbitrary"), name="moe_expert_down",
    )(tile_expert, n_used, hmid, w_down)


def _combine_kernel(pos_ref, y_hbm, rt_ref, res_ref, gate_ref, o_ref, buf_ref, sem):
    i = pl.program_id(0)
    tm = o_ref.shape[0]

    def issue(r, carry):
        for k in range(2):
            p = pos_ref[2 * (i * tm + r) + k]
            pltpu.make_async_copy(y_hbm.at[pl.ds(p, 1)], buf_ref.at[k, pl.ds(r, 1)], sem).start()
        return carry

    lax.fori_loop(0, tm, issue, 0, unroll=DMA_ISSUE_UNROLL)
    pltpu.make_async_copy(y_hbm.at[pl.ds(0, 2 * tm)], buf_ref.reshape(2 * tm, buf_ref.shape[2]), sem).wait()
    mix = rt_ref[:, 2:3] * buf_ref[0] + rt_ref[:, 3:4] * buf_ref[1]
    o_ref[...] = res_ref[...] + gate_ref[...] * mix


def _moe_combine(y_sorted, pos, route, res, res_tile_off, n_rows, gates, first_latent_tile):
    d = res.shape[1]
    tm = ROW_TILE
    sel = lambda i, p: (jnp.where(i + res_tile_off >= first_latent_tile, 1, 0), 0, 0)
    return pl.pallas_call(
        _combine_kernel,
        out_shape=jax.ShapeDtypeStruct((n_rows, d), F32),
        grid_spec=pltpu.PrefetchScalarGridSpec(
            num_scalar_prefetch=1, grid=(n_rows // tm,),
            in_specs=[pl.BlockSpec(memory_space=pl.ANY),
                      pl.BlockSpec((tm, ROUTER_LANES), lambda i, p: (i, 0)),
                      pl.BlockSpec((tm, d), lambda i, p: (i + res_tile_off, 0)),
                      pl.BlockSpec((None, 1, d), sel)],
            out_specs=pl.BlockSpec((tm, d), lambda i, p: (i, 0)),
            scratch_shapes=[pltpu.VMEM((2, tm, d), F32), pltpu.SemaphoreType.DMA(())]),
        compiler_params=_cparams("arbitrary", unchecked_dma=True), name="moe_combine",
    )(pos.reshape(-1), y_sorted, route, res, gates)


def _routing_plan(route, counts_row, n_tiles):
    tm = MOE_TILE
    n = route.shape[0]
    ids = route[:, 0:2].astype(jnp.int32)
    rank = route[:, 4:6].astype(jnp.int32)
    counts = counts_row[:MOE_EXPERTS].astype(jnp.int32)
    tiles = (counts + tm - 1) // tm
    tile_end = jnp.cumsum(tiles)
    tile_start = tile_end - tiles
    onehot = ids[:, :, None] == jnp.arange(MOE_EXPERTS)[None, None, :]
    pos = jnp.sum(jnp.where(onehot, tile_start[None, None, :] * tm, 0), axis=-1) + rank
    pos = pos.reshape(-1)
    n_used = tile_end[-1].astype(jnp.int32).reshape(1)
    tile_ids = jnp.arange(n_tiles, dtype=jnp.int32)
    tile_expert = jnp.minimum(jnp.sum((tile_end[None, :] <= tile_ids[:, None]).astype(jnp.int32), axis=1),
                              MOE_EXPERTS - 1)
    return pos.astype(jnp.int32), tile_expert, n_used


def _hier_moe(xt, tile_off, n_rows, norm_g, shifts, scales, gates, first_latent_tile, layer, moe):
    w_grp, b_grp, w_rt, b_rt, w_gate, w_up, w_down = moe
    hp, route, counts = _moe_router(xt, tile_off, n_rows, norm_g, shifts, scales, first_latent_tile,
                                    w_grp, b_grp, w_rt, b_rt)
    n_tiles = (2 * n_rows) // MOE_TILE + MOE_EXPERTS
    pos, tile_expert, n_used = _routing_plan(route, counts[0], n_tiles)
    xs = _moe_dispatch(hp, xt.shape[1] // 2 // LANES, pos, n_tiles)
    y_sorted = _moe_experts(xs, tile_expert, n_used, layer, w_gate, w_up, w_down)
    return _moe_combine(y_sorted, pos, route, xt, tile_off, n_rows, gates, first_latent_tile)


def kernel(x, c, ctx, c_ctx, mod_w, mod_b, norm1_g, norm2_g, ab_w_in, ab_q_norm, ab_k_norm, ab_sink, ab_pool_w, ab_pool_scale, ab_w_out, cd_w_in, s5_lam_re, s5_lam_im, s5_log_dt, s5_b_re, s5_b_im, s5_c_re, s5_c_im, s5_d, s5_glu_w, s5_glu_b, sg_ln_g, sg_ln_b, sg_w, sg_b, cd_w_out, moe_w_group, moe_b_group, moe_w_router, moe_b_router, moe_w_gate, moe_w_up, moe_w_down):
    batch, n_lat, d = x.shape
    n_ctx = ctx.shape[1]
    assert batch == 1 and n_ctx == ROW_TILE and n_lat % ROW_TILE == 0 and mod_w.shape[0] == 2
    ctx_tiles = n_ctx // ROW_TILE

    cvecs = jnp.concatenate([c_ctx[None, :], c, jnp.zeros((6, d), F32)], axis=0)
    mod = _modulation(cvecs, mod_w, mod_b)[:, 0:2, :].reshape(2, 2, 6, 1, d)
    xt = jnp.concatenate([ctx[0], x[0]], axis=0)

    def moe_params(layer):
        return (moe_w_group[layer], moe_b_group[layer], moe_w_router[layer], moe_b_router[layer],
                moe_w_gate, moe_w_up, moe_w_down)

    m0 = mod[0]
    sh1, sc1, g1, sh2, sc2, g2 = (m0[:, k] for k in range(6))
    h = _norm_modulate(xt, norm1_g[0], sh1, sc1)
    proj = _in_projection(h, ab_w_in[0])
    q, k, v = _qkv_prepare(proj, ab_q_norm[0], ab_k_norm[0], n_ctx)
    att = _attention(q, k, v, ab_sink[0], n_ctx)
    pooled = _pool_mixer(proj, Q_HEADS * HEAD_DIM + 2 * KV_HEADS * HEAD_DIM, ab_pool_w[0], ab_pool_scale[0], n_ctx)
    xt = _out_projection(att, pooled, ab_w_out[0], xt, g1, 0, lambda i: jnp.minimum(i, 1))
    xt = _hier_moe(xt, 0, n_ctx + n_lat, norm2_g[0], sh2, sc2, g2, ctx_tiles, 0, moe_params(0))

    m1 = mod[1]
    sh1, sc1, g1, sh2, sc2, g2 = (m1[:, k] for k in range(6))
    h = _norm_modulate(xt, norm1_g[1], sh1, sc1)
    proj = _in_projection(h, cd_w_in[0])
    cw = s5_d.shape[1]
    dw = sg_ln_g.shape[1]
    y_scan = _s5_scan(proj[:, :cw], n_ctx, s5_lam_re[0], s5_lam_im[0], s5_log_dt[0], s5_b_re[0], s5_b_im[0],
                      s5_c_re[0], s5_c_im[0])
    y_c = _s5_glu(y_scan, proj, s5_d[0], s5_glu_w[0], s5_glu_b[0], n_ctx)
    y_d = _gmlp_mixer(proj, cw, cw + dw, dw, sg_ln_g[0], sg_ln_b[0], sg_w[0], sg_b[0], n_ctx)
    xl = _out_projection(y_c, y_d, cd_w_out[0], xt, g1, ctx_tiles, lambda i: 1)
    out = _hier_moe(xl, 0, n_lat, norm2_g[1], sh2, sc2, g2, 0, 1, moe_params(1))
    return out[None]
```

```python
import functools
import math

import jax
import jax.numpy as jnp
from jax import lax
from jax.experimental import pallas as pl
from jax.experimental.pallas import tpu as pltpu

F32 = jnp.float32
BF16 = jnp.bfloat16

EPS = 1e-6
NEG_INF = -1e30
GRID_W = 64
HEAD_DIM = 128
Q_HEADS = 16
KV_HEADS = 4
Q_PER_KV = Q_HEADS // KV_HEADS
ATT_BLOCK = 128
ROPE_BASE = 10000.0
ROPE_PAIRS = HEAD_DIM // 4
POOL_WINDOWS = (2, 4, 8, 16)
POOL_HALO = 8
S5_H = 16
S5_P = 64
S5_CHUNK = 32
SG_HEADS = 8
SG_CHUNK = 128
MOE_GROUPS = 4
MOE_PER_GROUP = 8
MOE_EXPERTS = MOE_GROUPS * MOE_PER_GROUP
LANES = 128
ROUTER_LANES = LANES
ROW_TILE = 256
MOE_TILE = 256
VMEM_LIMIT = 56 * 1024 * 1024


DMA_ISSUE_UNROLL = 8


def _cparams(*sem, unchecked_dma=False):
    return pltpu.CompilerParams(dimension_semantics=sem, vmem_limit_bytes=VMEM_LIMIT,
                                disable_bounds_checks=unchecked_dma)


def _sigmoid(x):
    return 1.0 / (1.0 + jnp.exp(-x))


def _gelu(x):
    return 0.5 * x * (1.0 + jnp.tanh(math.sqrt(2.0 / math.pi) * (x + 0.044715 * (x * x * x))))


def _split3(x):
    hi = x.astype(BF16)
    r1 = x - hi.astype(F32)
    mid = r1.astype(BF16)
    lo = (r1 - mid.astype(F32)).astype(BF16)
    return hi, mid, lo


def _mod_kernel(c_ref, w_ref, b_ref, o_ref):
    cv = c_ref[...]
    s = (cv * _sigmoid(cv)).astype(BF16)
    o_ref[...] = jnp.dot(s, w_ref[...].astype(BF16), preferred_element_type=F32) + b_ref[...]


def _modulation(cvecs, mod_w, mod_b):
    depth, d, n = mod_w.shape
    tn = 512
    return pl.pallas_call(
        _mod_kernel,
        out_shape=jax.ShapeDtypeStruct((depth, 8, n), F32),
        grid=(depth, n // tn),
        in_specs=[pl.BlockSpec((8, d), lambda l, j: (0, 0)),
                  pl.BlockSpec((None, d, tn), lambda l, j: (l, 0, j)),
                  pl.BlockSpec((None, 1, tn), lambda l, j: (l, 0, j))],
        out_specs=pl.BlockSpec((None, 8, tn), lambda l, j: (l, 0, j)),
        compiler_params=_cparams("parallel", "parallel"),
        name="modulation",
    )(cvecs, mod_w, mod_b.reshape(depth, 1, n))


def _modulated_norm(x, g_ref, sh_ref, sc_ref):
    ms = jnp.mean(x * x, axis=-1, keepdims=True)
    return x * lax.rsqrt(ms + EPS) * g_ref[...] * (1.0 + sc_ref[...]) + sh_ref[...]


def _norm_kernel(x_ref, g_ref, sh_ref, sc_ref, o_ref):
    o_ref[...] = _modulated_norm(x_ref[...], g_ref, sh_ref, sc_ref).astype(BF16)


def _norm_modulate(xt, g, shifts, scales):
    t, d = xt.shape
    tm = ROW_TILE
    sel = lambda i: (jnp.minimum(i, 1), 0, 0)
    return pl.pallas_call(
        _norm_kernel,
        out_shape=jax.ShapeDtypeStruct((t, d), BF16),
        grid=(t // tm,),
        in_specs=[pl.BlockSpec((tm, d), lambda i: (i, 0)),
                  pl.BlockSpec((1, d), lambda i: (0, 0)),
                  pl.BlockSpec((None, 1, d), sel),
                  pl.BlockSpec((None, 1, d), sel)],
        out_specs=pl.BlockSpec((tm, d), lambda i: (i, 0)),
        compiler_params=_cparams("parallel"),
        name="norm_modulate",
    )(xt, g.reshape(1, d), shifts, scales)


def _inproj_kernel(h_ref, w_ref, o_ref, wbf_ref):
    @pl.when(pl.program_id(1) == 0)
    def _():
        wbf_ref[...] = w_ref[...].astype(BF16)

    o_ref[...] = jnp.dot(h_ref[...], wbf_ref[...], preferred_element_type=F32)


def _in_projection(h, w):
    t, d = h.shape
    n = w.shape[1]
    tm = 3 * ROW_TILE if t % (3 * ROW_TILE) == 0 else ROW_TILE
    tn = 512
    return pl.pallas_call(
        _inproj_kernel,
        out_shape=jax.ShapeDtypeStruct((t, n), F32),
        grid=(n // tn, t // tm),
        in_specs=[pl.BlockSpec((tm, d), lambda j, i: (i, 0)),
                  pl.BlockSpec((d, tn), lambda j, i: (0, j))],
        out_specs=pl.BlockSpec((tm, tn), lambda j, i: (i, j)),
        scratch_shapes=[pltpu.VMEM((d, tn), BF16)],
        compiler_params=_cparams("arbitrary", "arbitrary"),
        name="in_projection",
    )(h, w)


def _outproj_kernel(a1_ref, a2_ref, w_ref, res_ref, gate_ref, o_ref, wbf_ref, *, k1):
    @pl.when(pl.program_id(1) == 0)
    def _():
        wbf_ref[...] = w_ref[...].astype(BF16)

    acc = jnp.dot(a1_ref[...], wbf_ref[:k1, :], preferred_element_type=F32)
    acc = acc + jnp.dot(a2_ref[...], wbf_ref[k1:, :], preferred_element_type=F32)
    o_ref[...] = res_ref[...] + gate_ref[...] * acc


def _out_projection(a1, a2, w, res, gates, res_tile_off, gate_of_tile):
    r, k1 = a1.shape
    k2 = a2.shape[1]
    d = w.shape[1]
    tm, tn = ROW_TILE, 1024
    return pl.pallas_call(
        functools.partial(_outproj_kernel, k1=k1),
        out_shape=jax.ShapeDtypeStruct((r, d), F32),
        grid=(d // tn, r // tm),
        in_specs=[pl.BlockSpec((tm, k1), lambda j, i: (i, 0)),
                  pl.BlockSpec((tm, k2), lambda j, i: (i, 0)),
                  pl.BlockSpec((k1 + k2, tn), lambda j, i: (0, j)),
                  pl.BlockSpec((tm, tn), lambda j, i: (i + res_tile_off, j)),
                  pl.BlockSpec((None, 1, tn), lambda j, i: (gate_of_tile(i), 0, j))],
        out_specs=pl.BlockSpec((tm, tn), lambda j, i: (i, j)),
        scratch_shapes=[pltpu.VMEM((k1 + k2, tn), BF16)],
        compiler_params=_cparams("arbitrary", "arbitrary"),
        name="out_projection",
    )(a1, a2, w, res, gates)


def _qkv_kernel(q_ref, k_ref, v_ref, cos_ref, sa_ref, sb_ref, qg_ref, kg_ref, qo_ref, ko_ref, vo_ref):
    cos, sa, sb = cos_ref[...], sa_ref[...], sb_ref[...]

    def head(x, g, scale):
        ms = jnp.mean(x * x, axis=-1, keepdims=True)
        xn = x * lax.rsqrt(ms + EPS) * g
        r = xn * cos + pltpu.roll(xn, HEAD_DIM - ROPE_PAIRS, 1) * sa + pltpu.roll(xn, ROPE_PAIRS, 1) * sb
        return (r * scale).astype(BF16)

    for h in range(Q_HEADS):
        sl = slice(h * HEAD_DIM, (h + 1) * HEAD_DIM)
        qo_ref[:, sl] = head(q_ref[:, sl], qg_ref[...], HEAD_DIM ** -0.5)
    for h in range(KV_HEADS):
        sl = slice(h * HEAD_DIM, (h + 1) * HEAD_DIM)
        ko_ref[:, sl] = head(k_ref[:, sl], kg_ref[...], 1.0)
    vo_ref[...] = v_ref[...].astype(BF16)


def _rope_tables(n_ctx, n_lat):
    t = jnp.arange(n_lat)
    row = (t // GRID_W).astype(F32)
    col = (t % GRID_W).astype(F32)
    inv_freq = ROPE_BASE ** (-jnp.arange(ROPE_PAIRS, dtype=F32) / ROPE_PAIRS)
    ang_r = row[:, None] * inv_freq
    ang_c = col[:, None] * inv_freq
    ang = jnp.concatenate([ang_r, ang_r, ang_c, ang_c], axis=-1)
    ang = jnp.concatenate([jnp.zeros((n_ctx, HEAD_DIM), F32), ang], axis=0)
    cos, sin = jnp.cos(ang), jnp.sin(ang)
    first = (jnp.arange(HEAD_DIM) % (2 * ROPE_PAIRS)) < ROPE_PAIRS
    return cos, jnp.where(first, -sin, 0.0), jnp.where(first, 0.0, sin)


def _qkv_prepare(proj, q_g, k_g, n_ctx):
    t = proj.shape[0]
    tm = ROW_TILE
    qw, kw = Q_HEADS * HEAD_DIM, KV_HEADS * HEAD_DIM
    cos, sa, sb = _rope_tables(n_ctx, t - n_ctx)
    tab = pl.BlockSpec((tm, HEAD_DIM), lambda i: (i, 0))
    vec = pl.BlockSpec((1, HEAD_DIM), lambda i: (0, 0))
    return pl.pallas_call(
        _qkv_kernel,
        out_shape=(jax.ShapeDtypeStruct((t, qw), BF16), jax.ShapeDtypeStruct((t, kw), BF16),
                   jax.ShapeDtypeStruct((t, kw), BF16)),
        grid=(t // tm,),
        in_specs=[pl.BlockSpec((tm, qw), lambda i: (i, 0)),
                  pl.BlockSpec((tm, kw), lambda i: (i, qw // kw)),
                  pl.BlockSpec((tm, kw), lambda i: (i, qw // kw + 1)),
                  tab, tab, tab, vec, vec],
        out_specs=(pl.BlockSpec((tm, qw), lambda i: (i, 0)), pl.BlockSpec((tm, kw), lambda i: (i, 0)),
                   pl.BlockSpec((tm, kw), lambda i: (i, 0))),
        compiler_params=_cparams("parallel"),
        name="qkv_prepare",
    )(proj, proj, proj, cos, sa, sb, q_g.reshape(1, HEAD_DIM), k_g.reshape(1, HEAD_DIM))


def _attn_kernel(sink_ref, q_ref, kp_ref, kc_ref, kn_ref, kx_ref, vp_ref, vc_ref, vn_ref, vx_ref, o_ref,
                 *, ctx_blocks, n_blocks):
    b = pl.program_id(0)
    blk = ATT_BLOCK
    rows = Q_PER_KV * blk
    n_ctx = ctx_blocks * blk
    latent = b >= ctx_blocks
    lat_f = jnp.where(latent, 1.0, 0.0)
    prev_f = jnp.where(jnp.logical_and(latent, b - 1 >= ctx_blocks), 1.0, 0.0)
    next_f = jnp.where(jnp.logical_and(latent, b + 1 <= n_blocks - 1), 1.0, 0.0)
    qi = lax.broadcasted_iota(jnp.int32, (rows, 3 * blk + n_ctx), 0) % blk
    col = lax.broadcasted_iota(jnp.int32, (rows, 3 * blk + n_ctx), 1)
    kj = col % blk
    allowed = jnp.where(col < blk, jnp.where(kj >= qi, prev_f, 0.0),
                        jnp.where(col < 2 * blk, lat_f,
                                  jnp.where(col < 3 * blk, jnp.where(kj <= qi, next_f, 0.0), 1.0))) > 0.5
    for h in range(KV_HEADS):
        hs = slice(h * HEAD_DIM, (h + 1) * HEAD_DIM)
        q4 = jnp.concatenate([q_ref[:, (Q_PER_KV * h + g) * HEAD_DIM:(Q_PER_KV * h + g + 1) * HEAD_DIM]
                              for g in range(Q_PER_KV)], axis=0)
        kk = jnp.concatenate([kp_ref[:, hs], kc_ref[:, hs], kn_ref[:, hs], kx_ref[:, hs]], axis=0)
        vv = jnp.concatenate([vp_ref[:, hs], vc_ref[:, hs], vn_ref[:, hs], vx_ref[:, hs]], axis=0)
        s = lax.dot_general(q4, kk, (((1,), (1,)), ((), ())), preferred_element_type=F32)
        s = jnp.where(allowed, s, NEG_INF)
        sk = jnp.concatenate([jnp.full((blk, 1), sink_ref[Q_PER_KV * h + g], F32) for g in range(Q_PER_KV)],
                             axis=0)
        m = jnp.maximum(jnp.max(s, axis=-1, keepdims=True), sk)
        p = jnp.exp(s - m)
        den = jnp.sum(p, axis=-1, keepdims=True) + jnp.exp(sk - m)
        o = jnp.dot(p.astype(BF16), vv, preferred_element_type=F32) / den
        for g in range(Q_PER_KV):
            o_ref[:, (Q_PER_KV * h + g) * HEAD_DIM:(Q_PER_KV * h + g + 1) * HEAD_DIM] = (
                o[g * blk:(g + 1) * blk, :].astype(BF16))


def _attention(q, k, v, sink, n_ctx):
    t = q.shape[0]
    blk = ATT_BLOCK
    nb = t // blk
    cb = n_ctx // blk
    kw = k.shape[1]
    prev = pl.BlockSpec((blk, kw), lambda b, s: (jnp.maximum(b - 1, 0), 0))
    cur = pl.BlockSpec((blk, kw), lambda b, s: (b, 0))
    nxt = pl.BlockSpec((blk, kw), lambda b, s: (jnp.minimum(b + 1, nb - 1), 0))
    ctx = pl.BlockSpec((n_ctx, kw), lambda b, s: (0, 0))
    return pl.pallas_call(
        functools.partial(_attn_kernel, ctx_blocks=cb, n_blocks=nb),
        out_shape=jax.ShapeDtypeStruct(q.shape, BF16),
        grid_spec=pltpu.PrefetchScalarGridSpec(
            num_scalar_prefetch=1, grid=(nb,),
            in_specs=[pl.BlockSpec((blk, q.shape[1]), lambda b, s: (b, 0)),
                      prev, cur, nxt, ctx, prev, cur, nxt, ctx],
            out_specs=pl.BlockSpec((blk, q.shape[1]), lambda b, s: (b, 0))),
        compiler_params=_cparams("parallel"),
        name="window_attention",
    )(sink, q, k, k, k, k, v, v, v, v)


def _pool_kernel(*refs, n_ctx, n_lat, n_tiles):
    ng = len(POOL_WINDOWS)
    cur_refs, prev_refs, next_refs = refs[:ng], refs[ng:2 * ng], refs[2 * ng:3 * ng]
    w_ref, scale_ref, o_ref, xe_ref = refs[3 * ng:]
    i = pl.program_id(0)
    tm = ROW_TILE
    hl = POOL_HALO
    gw = w_ref.shape[1]
    prev_valid = i >= 2
    next_valid = jnp.logical_and(i >= 1, i < n_tiles - 1)
    pos = lax.broadcasted_iota(jnp.int32, (tm, 1), 0) + jnp.where(i == 0, 0, (i - 1) * tm)
    seg_len = jnp.where(i == 0, n_ctx, n_lat)
    for g, win in enumerate(POOL_WINDOWS):
        cur = cur_refs[g][...]
        xe_ref[0:hl, :] = jnp.where(prev_valid, prev_refs[g][...], 0.0)
        xe_ref[hl:hl + tm, :] = cur
        xe_ref[hl + tm:hl + tm + hl, :] = jnp.where(next_valid, next_refs[g][...], 0.0)
        acc = xe_ref[hl - win // 2:hl - win // 2 + tm, :]
        for o in range(1 - win // 2, win // 2):
            acc = acc + xe_ref[hl + o:hl + o + tm, :]
        cnt = jnp.minimum(pos - win // 2 + win, seg_len) - jnp.maximum(pos - win // 2, 0)
        d = acc / cnt.astype(F32) - cur
        y = jnp.dot(d.astype(BF16), w_ref[g].astype(BF16), preferred_element_type=F32)
        o_ref[:, g * gw:(g + 1) * gw] = (y * scale_ref[:, g * gw:(g + 1) * gw]).astype(BF16)


def _pool_mixer(proj, col0, w_pool, pool_scale, n_ctx):
    t = proj.shape[0]
    tm, hl = ROW_TILE, POOL_HALO
    ng, gw = w_pool.shape[0], w_pool.shape[1]
    nt = t // tm
    cb = col0 // gw
    per = tm // hl
    cur = [pl.BlockSpec((tm, gw), functools.partial(lambda i, g: (i, cb + g), g=g)) for g in range(ng)]
    prev = [pl.BlockSpec((hl, gw), functools.partial(lambda i, g: (jnp.maximum(i * per - 1, 0), cb + g), g=g))
            for g in range(ng)]
    nxt = [pl.BlockSpec((hl, gw), functools.partial(lambda i, g: (jnp.minimum((i + 1) * per, nt * per - 1), cb + g), g=g))
           for g in range(ng)]
    return pl.pallas_call(
        functools.partial(_pool_kernel, n_ctx=n_ctx, n_lat=t - n_ctx, n_tiles=nt),
        out_shape=jax.ShapeDtypeStruct((t, ng * gw), BF16),
        grid=(nt,),
        in_specs=cur + prev + nxt + [pl.BlockSpec((ng, gw, gw), lambda i: (0, 0, 0)),
                                     pl.BlockSpec((1, ng * gw), lambda i: (0, 0))],
        out_specs=pl.BlockSpec((tm, ng * gw), lambda i: (i, 0)),
        scratch_shapes=[pltpu.VMEM((tm + 2 * hl, gw), F32)],
        compiler_params=_cparams("parallel"),
        name="pool_mixer",
    )(*([proj] * (3 * ng)), w_pool, pool_scale.reshape(1, ng * gw))


def _s5_powers(lr, li, dt, kcol):
    mag = jnp.exp(kcol * (lr * dt))
    ang = kcol * (li * dt)
    return mag * jnp.cos(ang), mag * jnp.sin(ang)


def _s5_expand(z_re, z_im, e_re, e_im):
    n, p = e_re.shape
    hh = z_re.shape[0]
    er = jnp.broadcast_to(e_re[:, None, :], (n, hh, p)).reshape(n * hh, p)
    ei = jnp.broadcast_to(e_im[:, None, :], (n, hh, p)).reshape(n * hh, p)
    zr = jnp.broadcast_to(z_re[None, :, :], (n, hh, p)).reshape(n * hh, p)
    zi = jnp.broadcast_to(z_im[None, :, :], (n, hh, p)).reshape(n * hh, p)
    return zr * er - zi * ei, zr * ei + zi * er


def _s5_discretize_t(lr, li, dt, bt_re, bt_im):
    mag = jnp.exp(lr * dt)
    a_re, a_im = mag * jnp.cos(li * dt), mag * jnp.sin(li * dt)
    den = lr * lr + li * li
    n_re = a_re - 1.0
    k_re = (n_re * lr + a_im * li) / den
    k_im = (a_im * lr - n_re * li) / den
    return k_re * bt_re - k_im * bt_im, k_re * bt_im + k_im * bt_re


def _s5_group_params(lam_re_ref, lam_im_ref, ldt_ref, bt_re_ref, bt_im_ref, d):
    lr, li = lam_re_ref[d], lam_im_ref[d]
    dt = jnp.exp(ldt_ref[d])
    bb_re, bb_im = _s5_discretize_t(lr, li, dt, bt_re_ref[d], bt_im_ref[d])
    return lr, li, dt, bb_re, bb_im


def _s5_state_kernel(u_ref, lam_re_ref, lam_im_ref, ldt_ref, bt_re_ref, bt_im_ref, sf_ref, sb_ref):
    lc = S5_CHUNK
    u = u_ref[...].astype(BF16)
    t_col = lax.broadcasted_iota(jnp.int32, (lc, 1), 0).astype(F32)
    for d, out_ref, kcol in ((0, sf_ref, (lc - 1) - t_col), (1, sb_ref, t_col)):
        lr, li, dt, bb_re, bb_im = _s5_group_params(lam_re_ref, lam_im_ref, ldt_ref, bt_re_ref, bt_im_ref, d)
        e_re, e_im = _s5_powers(lr, li, dt, kcol)
        w_re, w_im = _s5_expand(bb_re, bb_im, e_re, e_im)
        w = jnp.concatenate([w_re, w_im], axis=1).astype(BF16)
        out_ref[...] = jnp.dot(u, w, preferred_element_type=F32)


def _s5_carry_kernel(sf_ref, sb_ref, lam_re_ref, lam_im_ref, ldt_ref, hf_ref, hb_ref, sw_ref, *, ctx_chunks):
    nc = sf_ref.shape[0]
    lanes = sf_ref.shape[1]
    first = (lax.broadcasted_iota(jnp.int32, (1, lanes), 1) % (2 * S5_P)) < S5_P

    def scan(d, s_ref, h_ref, order):
        lr, li, dt = lam_re_ref[d], lam_im_ref[d], jnp.exp(ldt_ref[d])
        mag = jnp.exp(S5_CHUNK * (lr * dt))
        ang = S5_CHUNK * (li * dt)
        pa = mag * jnp.cos(ang)
        a_im = mag * jnp.sin(ang)
        qa = jnp.where(first, -a_im, a_im)
        s_all = s_ref[...]
        sw_ref[...] = jnp.where(first, pltpu.roll(s_all, lanes - S5_P, 1), pltpu.roll(s_all, S5_P, 1))
        zero = jnp.zeros((1, lanes), F32)

        def run(lo, n, step, carry):
            def body(k, hc):
                h, hs = hc
                c = lo + k * step
                h_ref[pl.ds(c, 1), :] = h
                s = s_ref[pl.ds(c, 1), :]
                ss = sw_ref[pl.ds(c, 1), :]
                return pa * h + qa * hs + s, pa * hs - qa * h + ss
            return lax.fori_loop(0, n, body, carry)

        carry = (zero, zero)
        for lo, n, step in order:
            carry = run(lo, n, step, carry)

    scan(0, sf_ref, hf_ref, ((0, nc, 1),))
    scan(1, sb_ref, hb_ref, ((ctx_chunks - 1, ctx_chunks, -1), (nc - 1, nc - ctx_chunks, -1)))


def _s5_output_kernel(u_ref, hf_ref, hb_ref, lam_re_ref, lam_im_ref, ldt_ref, bt_re_ref, bt_im_ref,
                      c_re_ref, c_im_ref, y_ref):
    lc = S5_CHUNK
    half = lc // 2
    n = lc * S5_H
    u = u_ref[...].astype(BF16)
    t_col = lax.broadcasted_iota(jnp.int32, (lc, 1), 0).astype(F32)
    s_idx = lax.broadcasted_iota(jnp.int32, (n, n), 0) // S5_H
    t_idx = lax.broadcasted_iota(jnp.int32, (n, n), 1) // S5_H
    mt = jnp.zeros((n, n), F32)
    y = jnp.zeros((u.shape[0], n), F32)
    for d, h_ref in ((0, hf_ref), (1, hb_ref)):
        lr, li, dt, bb_re, bb_im = _s5_group_params(lam_re_ref, lam_im_ref, ldt_ref, bt_re_ref, bt_im_ref, d)
        c_re, c_im = c_re_ref[d], c_im_ref[d]
        if d == 0:
            kl, kr, kc = t_col - half, half - t_col, t_col + 1.0
            keep = s_idx <= t_idx
        else:
            kl, kr, kc = half - t_col, t_col - half, lc - t_col
            keep = s_idx >= t_idx
        l_re, l_im = _s5_expand(c_re, c_im, *_s5_powers(lr, li, dt, kl))
        r_re, r_im = _s5_expand(bb_re, bb_im, *_s5_powers(lr, li, dt, kr))
        lst = jnp.concatenate([l_re, -l_im], axis=1)
        rst = jnp.concatenate([r_re, r_im], axis=1)
        r3, l3 = _split3(rst), _split3(lst)
        m = jnp.zeros((n, n), F32)
        for ra, la in ((0, 0), (0, 1), (1, 0)):
            m = m + lax.dot_general(r3[ra], l3[la], (((1,), (1,)), ((), ())), preferred_element_type=F32)
        mt = mt + jnp.where(keep, m, 0.0)
        ca_re, ca_im = _s5_expand(c_re, c_im, *_s5_powers(lr, li, dt, kc))
        t2 = jnp.concatenate([ca_re, -ca_im], axis=1).astype(BF16)
        y = y + lax.dot_general(h_ref[...].astype(BF16), t2, (((1,), (1,)), ((), ())),
                                preferred_element_type=F32)
    y_ref[...] = y + jnp.dot(u, mt.astype(BF16), preferred_element_type=F32)


def _s5_scan(s_all, n_ctx, lam_re, lam_im, log_dt, b_re, b_im, c_re, c_im):
    t, cw = s_all.shape
    g, p, h, lc = cw // S5_H, S5_P, S5_H, S5_CHUNK
    nc = t // lc
    ug = s_all.reshape(nc, lc, g, h).transpose(2, 0, 1, 3).reshape(g, nc, lc * h)
    lam_re4 = lam_re.transpose(1, 0, 2)[:, :, None, :]
    lam_im4 = lam_im.transpose(1, 0, 2)[:, :, None, :]
    ldt4 = jnp.broadcast_to(log_dt.T[:, :, None, None], (g, 2, 1, p))
    bt_re = b_re.transpose(1, 0, 3, 2)
    bt_im = b_im.transpose(1, 0, 3, 2)
    c_re4 = c_re.transpose(1, 0, 2, 3)
    c_im4 = c_im.transpose(1, 0, 2, 3)
    vec4 = pl.BlockSpec((None, 2, 1, p), lambda i: (i, 0, 0, 0))
    mat4 = pl.BlockSpec((None, 2, h, p), lambda i: (i, 0, 0, 0))
    u_spec = pl.BlockSpec((None, nc, lc * h), lambda i: (i, 0, 0))
    st_spec = pl.BlockSpec((nc, 2 * p), lambda i: (0, i))
    st_shape = jax.ShapeDtypeStruct((nc, g * 2 * p), F32)
    s_f, s_b = pl.pallas_call(
        _s5_state_kernel, out_shape=(st_shape, st_shape), grid=(g,),
        in_specs=[u_spec, vec4, vec4, vec4, mat4, mat4],
        out_specs=(st_spec, st_spec),
        compiler_params=_cparams("parallel"), name="s5_chunk_states",
    )(ug, lam_re4, lam_im4, ldt4, bt_re, bt_im)

    def lanes(x):
        return jnp.concatenate([x, x], axis=-1).reshape(2, 1, g * 2 * p)
    cols = 8 * 2 * p
    row = pl.BlockSpec((2, 1, cols), lambda j: (0, 0, j))
    blk = pl.BlockSpec((nc, cols), lambda j: (0, j))
    h_f, h_b = pl.pallas_call(
        functools.partial(_s5_carry_kernel, ctx_chunks=n_ctx // lc),
        out_shape=(st_shape, st_shape), grid=(g * 2 * p // cols,),
        in_specs=[blk, blk, row, row, row], out_specs=(blk, blk),
        scratch_shapes=[pltpu.VMEM((nc, cols), F32)],
        compiler_params=_cparams("parallel"), name="s5_chunk_carry",
    )(s_f, s_b, lanes(lam_re), lanes(lam_im), lanes(jnp.broadcast_to(log_dt[:, :, None], (2, g, p))))

    yg = pl.pallas_call(
        _s5_output_kernel, out_shape=jax.ShapeDtypeStruct((g, nc, lc * h), F32), grid=(g,),
        in_specs=[u_spec, st_spec, st_spec, vec4, vec4, vec4, mat4, mat4, mat4, mat4],
        out_specs=u_spec,
        compiler_params=_cparams("parallel"), name="s5_outputs",
    )(ug, h_f, h_b, lam_re4, lam_im4, ldt4, bt_re, bt_im, c_re4, c_im4)
    return yg.reshape(g, nc, lc, h).transpose(1, 2, 0, 3).reshape(t, cw)


def _glu_kernel(y_ref, u_ref, dk_ref, w_ref, b_ref, o_ref, wbf_ref):
    @pl.when(pl.program_id(0) == 0)
    def _():
        wbf_ref[...] = w_ref[...].astype(BF16)

    yl = dk_ref[...] * u_ref[...] + y_ref[...]
    gl = _gelu(yl)
    z = jnp.dot(gl.astype(BF16), wbf_ref[...], preferred_element_type=F32) + b_ref[...]
    o_ref[...] = (gl * _sigmoid(z)).astype(BF16)


def _s5_glu(y_scan, proj, d_skip, glu_w, glu_b, n_ctx):
    t, cw = y_scan.shape
    tm = ROW_TILE
    off = n_ctx // tm
    n_lat = t - n_ctx
    vec = pl.BlockSpec((1, cw), lambda i: (0, 0))
    return pl.pallas_call(
        _glu_kernel, out_shape=jax.ShapeDtypeStruct((n_lat, cw), BF16), grid=(n_lat // tm,),
        in_specs=[pl.BlockSpec((tm, cw), lambda i: (i + off, 0)),
                  pl.BlockSpec((tm, cw), lambda i: (i + off, 0)),
                  vec, pl.BlockSpec((cw, cw), lambda i: (0, 0)), vec],
        out_specs=pl.BlockSpec((tm, cw), lambda i: (i, 0)),
        scratch_shapes=[pltpu.VMEM((cw, cw), BF16)],
        compiler_params=_cparams("arbitrary"), name="s5_glu",
    )(y_scan, proj, d_skip.reshape(1, cw), glu_w, glu_b.reshape(1, cw))


def _gmlp_kernel(*refs, n_col):
    u_refs, v_refs = refs[:n_col], refs[n_col:2 * n_col]
    lng_ref, lnb_ref, ws_ref, bst_ref, o_ref = refs[2 * n_col:]
    u = _gelu(jnp.concatenate([r[...] for r in u_refs], axis=1))
    v = _gelu(jnp.concatenate([r[...] for r in v_refs], axis=1))
    mu = jnp.mean(v, axis=-1, keepdims=True)
    vc = v - mu
    var = jnp.mean(vc * vc, axis=-1, keepdims=True)
    vn = (vc * lax.rsqrt(var + EPS) * lng_ref[...] + lnb_ref[...]).astype(BF16)
    gw = u.shape[1] // SG_HEADS
    for c in range(u.shape[0] // SG_CHUNK):
        rs = slice(c * SG_CHUNK, (c + 1) * SG_CHUNK)
        for g in range(SG_HEADS):
            cs = slice(g * gw, (g + 1) * gw)
            mixed = jnp.dot(ws_ref[g].astype(BF16), vn[rs, cs], preferred_element_type=F32)
            mixed = mixed + bst_ref[:, g:g + 1]
            o_ref[rs, cs] = (u[rs, cs] * mixed).astype(BF16)


def _gmlp_mixer(proj, col_u, col_v, width, ln_g, ln_b, w_s, b_s, n_ctx):
    t = proj.shape[0]
    tm, cwid = ROW_TILE, 1024
    off = n_ctx // tm
    n_lat = t - n_ctx
    n_col = width // cwid
    specs = [pl.BlockSpec((tm, cwid), functools.partial(lambda i, cb: (i + off, cb), cb=(c0 // cwid) + k))
             for c0 in (col_u, col_v) for k in range(n_col)]
    vec = pl.BlockSpec((1, width), lambda i: (0, 0))
    return pl.pallas_call(
        functools.partial(_gmlp_kernel, n_col=n_col),
        out_shape=jax.ShapeDtypeStruct((n_lat, width), BF16), grid=(n_lat // tm,),
        in_specs=specs + [vec, vec, pl.BlockSpec(w_s.shape, lambda i: (0, 0, 0)),
                          pl.BlockSpec((SG_CHUNK, SG_HEADS), lambda i: (0, 0))],
        out_specs=pl.BlockSpec((tm, width), lambda i: (i, 0)),
        compiler_params=_cparams("parallel"), name="gmlp_mixer",
    )(*([proj] * (2 * n_col)), ln_g.reshape(1, width), ln_b.reshape(1, width), w_s, b_s.T)


def _router_kernel(x_ref, g_ref, sh_ref, sc_ref, wr_ref, br_ref, hp_ref, rt_ref, cnt_ref, w3_ref, run_ref):
    @pl.when(pl.program_id(0) == 0)
    def _():
        hi, mid, lo = _split3(wr_ref[...])
        w3_ref[0], w3_ref[1], w3_ref[2] = hi, mid, lo
        run_ref[...] = jnp.zeros_like(run_ref)

    h = _modulated_norm(x_ref[...], g_ref, sh_ref, sc_ref)
    tm = h.shape[0]
    half = h.shape[1] // 2
    packed = pltpu.pack_elementwise([h[:, :half], h[:, half:]], packed_dtype=jnp.bfloat16)
    slab = half // LANES
    for s in range(slab):
        hp_ref[pl.ds(s, tm, stride=slab), :] = packed[:, s * LANES:(s + 1) * LANES]

    h3 = _split3(h)
    logits = br_ref[...]
    for a, b in ((0, 0), (0, 1), (1, 0)):
        logits = logits + jnp.dot(h3[a], w3_ref[b], preferred_element_type=F32)

    lane = lax.broadcasted_iota(jnp.int32, logits.shape, 1)
    big = jnp.int32(ROUTER_LANES)
    gl = jnp.where(lane < MOE_GROUPS, logits, NEG_INF)
    gm = jnp.max(gl, axis=-1, keepdims=True)
    g_w = 1.0 / jnp.sum(jnp.exp(gl - gm), axis=-1, keepdims=True)
    g_idx = jnp.min(jnp.where(gl == gm, lane, big), axis=-1, keepdims=True)
    lo_lane = MOE_GROUPS + g_idx * MOE_PER_GROUP
    fl = jnp.where(jnp.logical_and(lane >= lo_lane, lane < lo_lane + MOE_PER_GROUP), logits, NEG_INF)
    m1 = jnp.max(fl, axis=-1, keepdims=True)
    i1 = jnp.min(jnp.where(fl == m1, lane, big), axis=-1, keepdims=True)
    fl2 = jnp.where(lane == i1, NEG_INF, fl)
    m2 = jnp.max(fl2, axis=-1, keepdims=True)
    i2 = jnp.min(jnp.where(fl2 == m2, lane, big), axis=-1, keepdims=True)
    e2 = jnp.exp(m2 - m1)
    w1 = g_w / (1.0 + e2)
    w2 = g_w * e2 / (1.0 + e2)
    oh1 = jnp.where(lane == i1 - MOE_GROUPS, 1.0, 0.0)
    oh2 = jnp.where(lane == i2 - MOE_GROUPS, 1.0, 0.0)
    both = oh1 + oh2
    tri = jnp.where(lax.broadcasted_iota(jnp.int32, (tm, tm), 1) < lax.broadcasted_iota(jnp.int32, (tm, tm), 0),
                    1.0, 0.0).astype(BF16)
    before = jnp.dot(tri, both.astype(BF16), preferred_element_type=F32) + run_ref[...]
    r1 = jnp.sum(before * oh1, axis=-1, keepdims=True)
    r2 = jnp.sum(before * oh2, axis=-1, keepdims=True)
    run_ref[...] = run_ref[...] + jnp.sum(both, axis=0, keepdims=True)
    cnt_ref[...] = jnp.broadcast_to(run_ref[...], cnt_ref.shape)
    rt_ref[...] = jnp.where(lane == 0, (i1 - MOE_GROUPS).astype(F32),
                            jnp.where(lane == 1, (i2 - MOE_GROUPS).astype(F32),
                                      jnp.where(lane == 2, w1,
                                                jnp.where(lane == 3, w2,
                                                          jnp.where(lane == 4, r1, jnp.where(lane == 5, r2, 0.0))))))


def _moe_router(xt, tile_off, n_rows, g, shifts, scales, first_latent_tile, w_grp, b_grp, w_rt, b_rt):
    d = xt.shape[1]
    tm = ROW_TILE
    slab = d // 2 // LANES
    ne = MOE_GROUPS * MOE_PER_GROUP
    wr = jnp.concatenate([w_grp, w_rt.transpose(1, 0, 2).reshape(d, ne),
                          jnp.zeros((d, ROUTER_LANES - MOE_GROUPS - ne), F32)], axis=1)
    br = jnp.concatenate([b_grp, b_rt.reshape(ne), jnp.zeros((ROUTER_LANES - MOE_GROUPS - ne,), F32)])
    sel = lambda i: (jnp.where(i + tile_off >= first_latent_tile, 1, 0), 0, 0)
    return pl.pallas_call(
        _router_kernel,
        out_shape=(jax.ShapeDtypeStruct((n_rows * slab, LANES), jnp.uint32),
                   jax.ShapeDtypeStruct((n_rows, ROUTER_LANES), F32),
                   jax.ShapeDtypeStruct((8, ROUTER_LANES), F32)),
        grid=(n_rows // tm,),
        in_specs=[pl.BlockSpec((tm, d), lambda i: (i + tile_off, 0)),
                  pl.BlockSpec((1, d), lambda i: (0, 0)),
                  pl.BlockSpec((None, 1, d), sel), pl.BlockSpec((None, 1, d), sel),
                  pl.BlockSpec((d, ROUTER_LANES), lambda i: (0, 0)),
                  pl.BlockSpec((1, ROUTER_LANES), lambda i: (0, 0))],
        out_specs=(pl.BlockSpec((tm * slab, LANES), lambda i: (i, 0)),
                   pl.BlockSpec((tm, ROUTER_LANES), lambda i: (i, 0)),
                   pl.BlockSpec((8, ROUTER_LANES), lambda i: (0, 0))),
        scratch_shapes=[pltpu.VMEM((3, d, ROUTER_LANES), BF16), pltpu.VMEM((1, ROUTER_LANES), F32)],
        compiler_params=_cparams("arbitrary"), name="moe_router",
    )(xt, g.reshape(1, d), shifts, scales, wr, br.reshape(1, ROUTER_LANES))


def _row_scatter_kernel(pos_ref, hp_ref, init_hbm, out_hbm, sem, *, slab):
    del init_hbm
    i = pl.program_id(0)
    tm = hp_ref.shape[0] // slab

    def issue(r, carry):
        src = pl.multiple_of(r * slab, slab)
        for k in range(2):
            dst = pl.multiple_of(pos_ref[2 * (i * tm + r) + k] * slab, slab)
            pltpu.make_async_copy(hp_ref.at[pl.ds(src, slab)], out_hbm.at[pl.ds(dst, slab)], sem).start()
        return carry

    lax.fori_loop(0, tm, issue, 0, unroll=DMA_ISSUE_UNROLL)
    for _ in range(2):
        pltpu.make_async_copy(hp_ref, out_hbm.at[pl.ds(0, tm * slab)], sem).wait()


def _moe_dispatch(hp, slab, pos, n_tiles):
    tm = ROW_TILE
    n_rows = hp.shape[0] // slab
    zeros = jnp.zeros((n_tiles * MOE_TILE * slab, LANES), hp.dtype)
    return pl.pallas_call(
        functools.partial(_row_scatter_kernel, slab=slab),
        out_shape=jax.ShapeDtypeStruct(zeros.shape, hp.dtype),
        grid_spec=pltpu.PrefetchScalarGridSpec(
            num_scalar_prefetch=1, grid=(n_rows // tm,),
            in_specs=[pl.BlockSpec((tm * slab, LANES), lambda i, p: (i, 0)),
                      pl.BlockSpec(memory_space=pl.ANY)],
            out_specs=pl.BlockSpec(memory_space=pl.ANY),
            scratch_shapes=[pltpu.SemaphoreType.DMA(())]),
        input_output_aliases={2: 0},
        compiler_params=_cparams("arbitrary", unchecked_dma=True), name="moe_dispatch",
    )(pos, hp, zeros)


def _unpack_pairs(xp):
    lo = pltpu.unpack_elementwise(xp, index=0, packed_dtype=jnp.bfloat16, unpacked_dtype=F32)
    hi = pltpu.unpack_elementwise(xp, index=1, packed_dtype=jnp.bfloat16, unpacked_dtype=F32)
    return lo.astype(BF16), hi.astype(BF16)


def _expert_up_kernel(te_ref, nu_ref, xs_ref, wg_ref, wu_ref, h_ref, wgb_ref, wub_ref):
    t = pl.program_id(0)
    tc = jnp.minimum(t, nu_ref[0] - 1)
    new_expert = jnp.logical_or(t == 0, te_ref[tc] != te_ref[jnp.maximum(tc - 1, 0)])

    @pl.when(t < nu_ref[0])
    def _():
        @pl.when(new_expert)
        def _():
            wgb_ref[...] = wg_ref[...].astype(BF16)
            wub_ref[...] = wu_ref[...].astype(BF16)

        tm = h_ref.shape[0]
        slab = xs_ref.shape[0] // tm
        parts = [_unpack_pairs(xs_ref[pl.ds(s, tm, stride=slab), :]) for s in range(slab)]
        lo = jnp.concatenate([p[0] for p in parts], axis=1)
        hi = jnp.concatenate([p[1] for p in parts], axis=1)
        half = lo.shape[1]
        gate = (jnp.dot(lo, wgb_ref[:half, :], preferred_element_type=F32)
                + jnp.dot(hi, wgb_ref[half:, :], preferred_element_type=F32))
        up = (jnp.dot(lo, wub_ref[:half, :], preferred_element_type=F32)
              + jnp.dot(hi, wub_ref[half:, :], preferred_element_type=F32))
        h_ref[...] = (gate * _sigmoid(gate) * up).astype(BF16)


def _expert_down_kernel(te_ref, nu_ref, h_ref, wd_ref, y_ref, wdb_ref):
    t = pl.program_id(0)
    tc = jnp.minimum(t, nu_ref[0] - 1)
    new_expert = jnp.logical_or(t == 0, te_ref[tc] != te_ref[jnp.maximum(tc - 1, 0)])

    @pl.when(t < nu_ref[0])
    def _():
        @pl.when(new_expert)
        def _():
            wdb_ref[...] = wd_ref[...].astype(BF16)

        y_ref[...] = jnp.dot(h_ref[...], wdb_ref[...], preferred_element_type=F32)


def _moe_experts(xs, tile_expert, n_used, layer, w_gate, w_up, w_down):
    tm = MOE_TILE
    d, ff = w_gate.shape[2], w_gate.shape[3]
    in_slab = d // 2 // LANES
    npad = xs.shape[0] // in_slab
    nt = npad // tm
    tile = lambda t, nu: jnp.minimum(t, nu[0] - 1)
    hmid = pl.pallas_call(
        _expert_up_kernel,
        out_shape=jax.ShapeDtypeStruct((npad, ff), BF16),
        grid_spec=pltpu.PrefetchScalarGridSpec(
            num_scalar_prefetch=2, grid=(nt,),
            in_specs=[pl.BlockSpec((tm * in_slab, LANES), lambda t, te, nu: (tile(t, nu), 0)),
                      pl.BlockSpec((None, None, d, ff), lambda t, te, nu: (layer, te[tile(t, nu)], 0, 0)),
                      pl.BlockSpec((None, None, d, ff), lambda t, te, nu: (layer, te[tile(t, nu)], 0, 0))],
            out_specs=pl.BlockSpec((tm, ff), lambda t, te, nu: (tile(t, nu), 0)),
            scratch_shapes=[pltpu.VMEM((d, ff), BF16), pltpu.VMEM((d, ff), BF16)]),
        compiler_params=_cparams("arbitrary"), name="moe_expert_up",
    )(tile_expert, n_used, xs, w_gate, w_up)
    return pl.pallas_call(
        _expert_down_kernel,
        out_shape=jax.ShapeDtypeStruct((npad, d), F32),
        grid_spec=pltpu.PrefetchScalarGridSpec(
            num_scalar_prefetch=2, grid=(nt,),
            in_specs=[pl.BlockSpec((tm, ff), lambda t, te, nu: (tile(t, nu), 0)),
                      pl.BlockSpec((None, None, ff, d), lambda t, te, nu: (layer, te[tile(t, nu)], 0, 0))],
            out_specs=pl.BlockSpec((tm, d), lambda t, te, nu: (tile(t, nu), 0)),
            scratch_shapes=[pltpu.VMEM((ff, d), BF16)]),
        compiler_params=_cparams("arbitrary"), name="moe_expert_down",
    )(tile_expert, n_used, hmid, w_down)


def _combine_kernel(pos_ref, y_hbm, rt_ref, res_ref, gate_ref, o_ref, buf_ref, sem):
    i = pl.program_id(0)
    tm = o_ref.shape[0]

    def issue(r, carry):
        for k in range(2):
            p = pos_ref[2 * (i * tm + r) + k]
            pltpu.make_async_copy(y_hbm.at[pl.ds(p, 1)], buf_ref.at[k, pl.ds(r, 1)], sem).start()
        return carry

    lax.fori_loop(0, tm, issue, 0, unroll=DMA_ISSUE_UNROLL)
    pltpu.make_async_copy(y_hbm.at[pl.ds(0, 2 * tm)], buf_ref.reshape(2 * tm, buf_ref.shape[2]), sem).wait()
    mix = rt_ref[:, 2:3] * buf_ref[0] + rt_ref[:, 3:4] * buf_ref[1]
    o_ref[...] = res_ref[...] + gate_ref[...] * mix


def _moe_combine(y_sorted, pos, route, res, res_tile_off, n_rows, gates, first_latent_tile):
    d = res.shape[1]
    tm = ROW_TILE
    sel = lambda i, p: (jnp.where(i + res_tile_off >= first_latent_tile, 1, 0), 0, 0)
    return pl.pallas_call(
        _combine_kernel,
        out_shape=jax.ShapeDtypeStruct((n_rows, d), F32),
        grid_spec=pltpu.PrefetchScalarGridSpec(
            num_scalar_prefetch=1, grid=(n_rows // tm,),
            in_specs=[pl.BlockSpec(memory_space=pl.ANY),
                      pl.BlockSpec((tm, ROUTER_LANES), lambda i, p: (i, 0)),
                      pl.BlockSpec((tm, d), lambda i, p: (i + res_tile_off, 0)),
                      pl.BlockSpec((None, 1, d), sel)],
            out_specs=pl.BlockSpec((tm, d), lambda i, p: (i, 0)),
            scratch_shapes=[pltpu.VMEM((2, tm, d), F32), pltpu.SemaphoreType.DMA(())]),
        compiler_params=_cparams("arbitrary", unchecked_dma=True), name="moe_combine",
    )(pos.reshape(-1), y_sorted, route, res, gates)


def _routing_plan(route, counts_row, n_tiles):
    tm = MOE_TILE
    n = route.shape[0]
    ids = route[:, 0:2].astype(jnp.int32)
    rank = route[:, 4:6].astype(jnp.int32)
    counts = counts_row[:MOE_EXPERTS].astype(jnp.int32)
    tiles = (counts + tm - 1) // tm
    tile_end = jnp.cumsum(tiles)
    tile_start = tile_end - tiles
    onehot = ids[:, :, None] == jnp.arange(MOE_EXPERTS)[None, None, :]
    pos = jnp.sum(jnp.where(onehot, tile_start[None, None, :] * tm, 0), axis=-1) + rank
    pos = pos.reshape(-1)
    n_used = tile_end[-1].astype(jnp.int32).reshape(1)
    tile_ids = jnp.arange(n_tiles, dtype=jnp.int32)
    tile_expert = jnp.minimum(jnp.sum((tile_end[None, :] <= tile_ids[:, None]).astype(jnp.int32), axis=1),
                              MOE_EXPERTS - 1)
    return pos.astype(jnp.int32), tile_expert, n_used


def _hier_moe(xt, tile_off, n_rows, norm_g, shifts, scales, gates, first_latent_tile, layer, moe):
    w_grp, b_grp, w_rt, b_rt, w_gate, w_up, w_down = moe
    hp, route, counts = _moe_router(xt, tile_off, n_rows, norm_g, shifts, scales, first_latent_tile,
                                    w_grp, b_grp, w_rt, b_rt)
    n_tiles = (2 * n_rows) // MOE_TILE + MOE_EXPERTS
    pos, tile_expert, n_used = _routing_plan(route, counts[0], n_tiles)
    xs = _moe_dispatch(hp, xt.shape[1] // 2 // LANES, pos, n_tiles)
    y_sorted = _moe_experts(xs, tile_expert, n_used, layer, w_gate, w_up, w_down)
    return _moe_combine(y_sorted, pos, route, xt, tile_off, n_rows, gates, first_latent_tile)


def kernel(x, c, ctx, c_ctx, mod_w, mod_b, norm1_g, norm2_g, ab_w_in, ab_q_norm, ab_k_norm, ab_sink, ab_pool_w, ab_pool_scale, ab_w_out, cd_w_in, s5_lam_re, s5_lam_im, s5_log_dt, s5_b_re, s5_b_im, s5_c_re, s5_c_im, s5_d, s5_glu_w, s5_glu_b, sg_ln_g, sg_ln_b, sg_w, sg_b, cd_w_out, moe_w_group, moe_b_group, moe_w_router, moe_b_router, moe_w_gate, moe_w_up, moe_w_down):
    batch, n_lat, d = x.shape
    n_ctx = ctx.shape[1]
    assert batch == 1 and n_ctx == ROW_TILE and n_lat % ROW_TILE == 0 and mod_w.shape[0] == 2
    ctx_tiles = n_ctx // ROW_TILE

    cvecs = jnp.concatenate([c_ctx[None, :], c, jnp.zeros((6, d), F32)], axis=0)
    mod = _modulation(cvecs, mod_w, mod_b)[:, 0:2, :].reshape(2, 2, 6, 1, d)
    xt = jnp.concatenate([ctx[0], x[0]], axis=0)

    def moe_params(layer):
        return (moe_w_group[layer], moe_b_group[layer], moe_w_router[layer], moe_b_router[layer],
                moe_w_gate, moe_w_up, moe_w_down)

    m0 = mod[0]
    sh1, sc1, g1, sh2, sc2, g2 = (m0[:, k] for k in range(6))
    h = _norm_modulate(xt, norm1_g[0], sh1, sc1)
    proj = _in_projection(h, ab_w_in[0])
    q, k, v = _qkv_prepare(proj, ab_q_norm[0], ab_k_norm[0], n_ctx)
    att = _attention(q, k, v, ab_sink[0], n_ctx)
    pooled = _pool_mixer(proj, Q_HEADS * HEAD_DIM + 2 * KV_HEADS * HEAD_DIM, ab_pool_w[0], ab_pool_scale[0], n_ctx)
    xt = _out_projection(att, pooled, ab_w_out[0], xt, g1, 0, lambda i: jnp.minimum(i, 1))
    xt = _hier_moe(xt, 0, n_ctx + n_lat, norm2_g[0], sh2, sc2, g2, ctx_tiles, 0, moe_params(0))

    m1 = mod[1]
    sh1, sc1, g1, sh2, sc2, g2 = (m1[:, k] for k in range(6))
    h = _norm_modulate(xt, norm1_g[1], sh1, sc1)
    proj = _in_projection(h, cd_w_in[0])
    cw = s5_d.shape[1]
    dw = sg_ln_g.shape[1]
    y_scan = _s5_scan(proj[:, :cw], n_ctx, s5_lam_re[0], s5_lam_im[0], s5_log_dt[0], s5_b_re[0], s5_b_im[0],
                      s5_c_re[0], s5_c_im[0])
    y_c = _s5_glu(y_scan, proj, s5_d[0], s5_glu_w[0], s5_glu_b[0], n_ctx)
    y_d = _gmlp_mixer(proj, cw, cw + dw, dw, sg_ln_g[0], sg_ln_b[0], sg_w[0], sg_b[0], n_ctx)
    xl = _out_projection(y_c, y_d, cd_w_out[0], xt, g1, ctx_tiles, lambda i: 1)
    out = _hier_moe(xl, 0, n_lat, norm2_g[1], sh2, sc2, g2, 0, 1, moe_params(1))
    return out[None]
```
